```python
import jax, jax.numpy as jnp
from jax import lax
import numpy as np

D_MODEL = 2048
BATCH = 4
SEQ = 2048
DEPTH = 2
DEC_BATCH = 128
DEC_SEQ = 1
PAST_LEN = 16384
PAGE_SIZE = 128

D_CONV_A = D_MODEL // 2
D_SHORT = D_MODEL - D_CONV_A
MIX_WIDTH = D_CONV_A + D_SHORT
N_GROUPS_A = 8
N_GROUPS_B = 8
CONV_A_WIDTH = 31
SHORT_WIDTH = 3
IN_COLS = 2 * D_CONV_A + 3 * D_SHORT
D_FF_DENSE = (11 * D_MODEL) // 4
N_EXPERTS = 8
TOP_K = 2
D_FF_EXPERT = (7 * D_MODEL) // 2
RMS_EPS = 1e-6
LN_EPS = 1e-5

kernel_name = "hybrid_conformer_shortconv_moe_step"


def _rmsnorm(x, g):
    xf = x.astype(jnp.float32)
    y = xf * lax.rsqrt(jnp.mean(xf * xf, axis=-1, keepdims=True) + RMS_EPS)
    return (y * g.astype(jnp.float32)).astype(x.dtype)


def _layernorm(x, g, b):
    xf = x.astype(jnp.float32)
    mu = jnp.mean(xf, axis=-1, keepdims=True)
    var = jnp.mean(jnp.square(xf - mu), axis=-1, keepdims=True)
    y = (xf - mu) * lax.rsqrt(var + LN_EPS)
    return (y * g.astype(jnp.float32) + b.astype(jnp.float32)).astype(x.dtype)


def _causal_dwconv(u, w, past):
    k, c = w.shape
    full = jnp.concatenate([past.astype(u.dtype), u], axis=1)
    out = lax.conv_general_dilated(
        full, w.astype(u.dtype)[:, None, :], window_strides=(1,), padding="VALID",
        dimension_numbers=("NWC", "WIO", "NWC"), feature_group_count=c)
    return out, full[:, full.shape[1] - (k - 1):]


def _mixer(h, past_a, past_b, w_in, conv_a_w, conv_a_b, ln_a_g, ln_a_b, conv_b_w, w_out):
    proj = jnp.einsum("btd,dc->btc", h, w_in)
    o1 = D_CONV_A
    o2 = 2 * D_CONV_A
    o3 = o2 + D_SHORT
    o4 = o3 + D_SHORT
    a_val, a_gate = proj[..., :o1], proj[..., o1:o2]
    b_gate, c_gate, h_b = proj[..., o2:o3], proj[..., o3:o4], proj[..., o4:]
    glu = a_val * jax.nn.sigmoid(a_gate)
    conv_a, new_a = _causal_dwconv(glu, conv_a_w, past_a)
    y_a = jax.nn.silu(_layernorm(conv_a + conv_a_b, ln_a_g, ln_a_b))
    u = c_gate * h_b
    conv_b, new_b = _causal_dwconv(u, conv_b_w, past_b)
    y_b = b_gate * conv_b
    y = jnp.concatenate([y_a, y_b], axis=-1)
    return jnp.einsum("btm,md->btd", y, w_out), new_a, new_b


def _swiglu(h, w_gate, w_up, w_down):
    g = jnp.einsum("...d,df->...f", h, w_gate)
    up = jnp.einsum("...d,df->...f", h, w_up)
    return jnp.einsum("...f,fd->...d", jax.nn.silu(g) * up, w_down)


def _moe(h, router, w_gate, w_up, w_down):
    b, t, d = h.shape
    h2 = h.reshape(b * t, d)
    logits = jnp.einsum("nd,de->ne", h2.astype(jnp.float32), router.astype(jnp.float32))
    top_vals, top_idx = lax.top_k(logits, TOP_K)
    gates = jax.nn.softmax(top_vals, axis=-1)
    weights = jnp.sum(jax.nn.one_hot(top_idx, N_EXPERTS, dtype=jnp.float32) * gates[..., None], axis=1)
    out = jnp.zeros_like(h2)
    for e in range(N_EXPERTS):
        out = out + weights[:, e:e + 1].astype(h2.dtype) * _swiglu(h2, w_gate[e], w_up[e], w_down[e])
    return out.reshape(b, t, d)


def _trunk(x, past_a, past_b, mix_norm, w_in, conv_a_w, conv_a_b, ln_a_g, ln_a_b, conv_b_w, w_out,
           ffn_norm, dense_w_gate, dense_w_up, dense_w_down, router, exp_w_gate, exp_w_up, exp_w_down,
           final_norm):
    new_a, new_b = [], []
    for l in range(DEPTH):
        h = _rmsnorm(x, mix_norm[l])
        mix, sa, sb = _mixer(h, past_a[l], past_b[l], w_in[l], conv_a_w[l], conv_a_b[l],
                             ln_a_g[l], ln_a_b[l], conv_b_w[l], w_out[l])
        x = x + mix
        h = _rmsnorm(x, ffn_norm[l])
        j = l // 2
        if l % 2 == 0:
            x = x + _swiglu(h, dense_w_gate[j], dense_w_up[j], dense_w_down[j])
        else:
            x = x + _moe(h, router[j], exp_w_gate[j], exp_w_up[j], exp_w_down[j])
        new_a.append(sa)
        new_b.append(sb)
    return _rmsnorm(x, final_norm), jnp.stack(new_a), jnp.stack(new_b)


def setup_inputs(seed: int = 0) -> dict:
    key = jax.random.key(seed)
    ks = jax.random.split(key, 24)
    n_dense = (DEPTH + 1) // 2
    n_moe = DEPTH // 2
    f32 = jnp.float32

    def nrm(k, shape, fan):
        return jax.random.normal(k, shape, f32) * (fan ** -0.5)

    def gain(k, shape):
        return 1.0 + 0.01 * jax.random.normal(k, shape, f32)

    return {
        "x_prompt": jax.random.normal(ks[0], (BATCH, SEQ, D_MODEL), f32),
        "x_sample": jax.random.normal(ks[1], (DEC_BATCH, DEC_SEQ, D_MODEL), f32),
        "state_conv_a": 0.5 * jax.random.normal(ks[2], (DEPTH, DEC_BATCH, CONV_A_WIDTH - 1, D_CONV_A), f32),
        "state_short": 0.5 * jax.random.normal(ks[3], (DEPTH, DEC_BATCH, SHORT_WIDTH - 1, D_SHORT), f32),
        "mix_norm": gain(ks[4], (DEPTH, D_MODEL)),
        "w_in": nrm(ks[5], (DEPTH, D_MODEL, IN_COLS), D_MODEL),
        "conv_a_w": nrm(ks[6], (DEPTH, CONV_A_WIDTH, D_CONV_A), CONV_A_WIDTH),
        "conv_a_b": 0.01 * jax.random.normal(ks[7], (DEPTH, D_CONV_A), f32),
        "ln_a_g": gain(ks[8], (DEPTH, D_CONV_A)),
        "ln_a_b": 0.01 * jax.random.normal(ks[9], (DEPTH, D_CONV_A), f32),
        "conv_b_w": nrm(ks[10], (DEPTH, SHORT_WIDTH, D_SHORT), SHORT_WIDTH),
        "w_out": nrm(ks[11], (DEPTH, MIX_WIDTH, D_MODEL), MIX_WIDTH),
        "ffn_norm": gain(ks[12], (DEPTH, D_MODEL)),
        "dense_w_gate": nrm(ks[13], (n_dense, D_MODEL, D_FF_DENSE), D_MODEL),
        "dense_w_up": nrm(ks[14], (n_dense, D_MODEL, D_FF_DENSE), D_MODEL),
        "dense_w_down": nrm(ks[15], (n_dense, D_FF_DENSE, D_MODEL), D_FF_DENSE),
        "router": nrm(ks[16], (n_moe, D_MODEL, N_EXPERTS), D_MODEL),
        "exp_w_gate": nrm(ks[17], (n_moe, N_EXPERTS, D_MODEL, D_FF_EXPERT), D_MODEL),
        "exp_w_up": nrm(ks[18], (n_moe, N_EXPERTS, D_MODEL, D_FF_EXPERT), D_MODEL),
        "exp_w_down": nrm(ks[19], (n_moe, N_EXPERTS, D_FF_EXPERT, D_MODEL), D_FF_EXPERT),
        "final_norm": gain(ks[20], (D_MODEL,)),
    }


def reference(x_prompt, x_sample, state_conv_a, state_short, mix_norm, w_in, conv_a_w, conv_a_b,
              ln_a_g, ln_a_b, conv_b_w, w_out, ffn_norm, dense_w_gate, dense_w_up, dense_w_down,
              router, exp_w_gate, exp_w_up, exp_w_down, final_norm):
    weights = (mix_norm, w_in, conv_a_w, conv_a_b, ln_a_g, ln_a_b, conv_b_w, w_out, ffn_norm,
               dense_w_gate, dense_w_up, dense_w_down, router, exp_w_gate, exp_w_up, exp_w_down,
               final_norm)
    zero_a = jnp.zeros((DEPTH, x_prompt.shape[0], CONV_A_WIDTH - 1, D_CONV_A), x_prompt.dtype)
    zero_b = jnp.zeros((DEPTH, x_prompt.shape[0], SHORT_WIDTH - 1, D_SHORT), x_prompt.dtype)
    y_prompt, new_conv_a_prompt, new_short_prompt = _trunk(x_prompt, zero_a, zero_b, *weights)
    y_sample, new_conv_a_sample, new_short_sample = _trunk(x_sample, state_conv_a, state_short, *weights)
    return (y_prompt, y_sample, new_conv_a_prompt, new_short_prompt, new_conv_a_sample, new_short_sample)
```

```python
import functools

import jax
import jax.numpy as jnp
from jax import lax
from jax.experimental import pallas as pl
from jax.experimental.pallas import tpu as pltpu

F32 = jnp.float32
BF16 = jnp.bfloat16

RMS_EPS = 1e-6
LN_EPS = 1e-5
TOP_K = 2

SUBLANES = 8
VMEM_LIMIT_BYTES = 56 * 1024 * 1024

IN_ROWS = 512
IN_COLS = 256
MIX_ROWS = 256
CONV_CHUNK = 16
SAMPLE_SEQS = 16
SAMPLE_K = 256
DENSE_ROWS = 1024
FF_COLS = 256
MOE_ROWS = 512
GATHER_ROWS = 256
COMBINE_ROWS = 128


def _params(*sem):
    return pltpu.CompilerParams(dimension_semantics=sem, vmem_limit_bytes=VMEM_LIMIT_BYTES)


def _rms(x, gain):
    y = x * lax.rsqrt(jnp.mean(x * x, axis=-1, keepdims=True) + RMS_EPS)
    return y * gain


def _dot(a, b):
    return jnp.dot(a, b, preferred_element_type=F32)


def _split(v):
    hi = v.astype(BF16)
    return hi, (v - hi.astype(F32)).astype(BF16)


def _dot3(a, b):
    a_hi, a_lo = a
    b_hi, b_lo = b
    return _dot(a_hi, b_hi) + (_dot(a_lo, b_hi) + _dot(a_hi, b_lo))


def _in_proj_kernel(xp_ref, xs_ref, g_ref, w0, w1, w2, w3, w4,
                    glu_p, u_p, bg_p, glu_s, u_s, bg_s, whi_ref, wlo_ref, *, tc, n_pt):
    i = pl.program_id(1)

    @pl.when(i == 0)
    def _():
        for k, w in enumerate((w0, w1, w2, w3, w4)):
            hi, lo = _split(w[...])
            whi_ref[:, k * tc:(k + 1) * tc] = hi
            wlo_ref[:, k * tc:(k + 1) * tc] = lo

    def emit(proj, glu_ref, u_ref, bg_ref):
        glu_ref[...] = proj[:, 0:tc] * jax.nn.sigmoid(proj[:, tc:2 * tc])
        bg_ref[...] = proj[:, 2 * tc:3 * tc]
        u_ref[...] = proj[:, 3 * tc:4 * tc] * proj[:, 4 * tc:5 * tc]

    @pl.when(i < n_pt)
    def _():
        h = _rms(xp_ref[...], g_ref[...]).astype(BF16)
        emit(_dot(h, whi_ref[...]), glu_p, u_p, bg_p)

    @pl.when(i == n_pt)
    def _():
        h = _split(_rms(xs_ref[...], g_ref[...]))
        emit(_dot3(h, (whi_ref[...], wlo_ref[...])), glu_s, u_s, bg_s)


def _in_proj(x_p, x_s, gain, w_in, layer, d_half):
    n_p, d = x_p.shape
    n_s = x_s.shape[0]
    tm, tc = IN_ROWS, IN_COLS
    nct, n_pt = d_half // tc, n_p // tm
    w_specs = [
        pl.BlockSpec((None, d, tc), functools.partial(lambda c, i, k: (layer, 0, k * nct + c), k=k))
        for k in range(5)
    ]
    prow = lambda c, i: jnp.minimum(i, n_pt - 1)
    out_p = jax.ShapeDtypeStruct((n_p, d_half), F32)
    out_s = jax.ShapeDtypeStruct((n_s, d_half), F32)
    op_spec = pl.BlockSpec((tm, tc), lambda c, i: (prow(c, i), c))
    os_spec = pl.BlockSpec((n_s, tc), lambda c, i: (0, c))
    return pl.pallas_call(
        functools.partial(_in_proj_kernel, tc=tc, n_pt=n_pt),
        grid=(nct, n_pt + 1),
        in_specs=[pl.BlockSpec((tm, d), lambda c, i: (prow(c, i), 0)),
                  pl.BlockSpec((n_s, d), lambda c, i: (0, 0)),
                  pl.BlockSpec((None, 1, d), lambda c, i: (layer, 0, 0))] + w_specs,
        out_specs=[op_spec, op_spec, op_spec, os_spec, os_spec, os_spec],
        out_shape=[out_p, out_p, out_p, out_s, out_s, out_s],
        scratch_shapes=[pltpu.VMEM((d, 5 * tc), BF16), pltpu.VMEM((d, 5 * tc), BF16)],
        compiler_params=_params("arbitrary", "arbitrary"),
        name="in_proj",
    )(x_p, x_s, gain, w_in, w_in, w_in, w_in, w_in)


def _cast_kernel(w_ref, o_ref):
    o_ref[...] = w_ref[...].astype(o_ref.dtype)


def _to_bf16(w):
    l, r, c = w.shape
    tr = min(r, 512)
    spec = pl.BlockSpec((None, tr, c), lambda i, j: (i, j, 0))
    return pl.pallas_call(
        _cast_kernel, grid=(l, r // tr), in_specs=[spec], out_specs=spec,
        out_shape=jax.ShapeDtypeStruct(w.shape, BF16),
        compiler_params=_params("arbitrary", "arbitrary"), name="cast_bf16",
    )(w)


def _route(logits, idx_ref, gate_ref):
    n_e = logits.shape[-1]
    lane = lax.broadcasted_iota(jnp.int32, logits.shape, 1)
    m1 = jnp.max(logits, axis=-1, keepdims=True)
    i1 = jnp.min(jnp.where(logits == m1, lane, n_e), axis=-1, keepdims=True)
    rest = jnp.where(lane == i1, -jnp.inf, logits)
    m2 = jnp.max(rest, axis=-1, keepdims=True)
    i2 = jnp.min(jnp.where(rest == m2, lane, n_e), axis=-1, keepdims=True)
    e2 = jnp.exp(m2 - m1)
    den = 1.0 + e2
    col = lax.broadcasted_iota(jnp.int32, (logits.shape[0], TOP_K), 1)
    idx_ref[...] = jnp.where(col == 0, i1, i2)
    gate_ref[...] = jnp.where(col == 0, 1.0 / den, e2 / den)


def _ln_silu(conv, cb_ref, lg_ref, lb_ref):
    v = conv + cb_ref[...]
    mu = jnp.mean(v, axis=-1, keepdims=True)
    var = jnp.mean(jnp.square(v - mu), axis=-1, keepdims=True)
    z = (v - mu) * lax.rsqrt(var + LN_EPS) * lg_ref[...] + lb_ref[...]
    return z * jax.nn.sigmoid(z)


def _shift_phases(offset, taps):
    return sorted({(offset + k) % SUBLANES for k in range(taps)} - {0})


def _mixer_prompt_kernel(*refs, tm, ka, kb, halo_a, halo_b, routed):
    (glu_ref, gluh_ref, u_ref, uh_ref, bg_ref, x_ref, caw_ref, cab_ref, lg_ref, lb_ref, cbw_ref,
     wout_ref) = refs[:12]
    refs = refs[12:]
    if routed:
        fg_ref, router_ref, x1_ref, h2_ref, idx_ref, gate_ref = refs[:6]
        refs = refs[6:]
    else:
        x1_ref = refs[0]
        refs = refs[1:]
    win_ref, sha_ref, uwin_ref, shb_ref, y_ref = refs
    t = pl.program_id(1)
    da = glu_ref.shape[-1]

    first = t == 0
    win_ref[0:halo_a, :] = jnp.where(first, 0.0, gluh_ref[...])
    win_ref[halo_a:halo_a + tm, :] = glu_ref[...]
    uwin_ref[0:halo_b, :] = jnp.where(first, 0.0, uh_ref[...])
    uwin_ref[halo_b:halo_b + tm, :] = u_ref[...]

    off_a = halo_a - (ka - 1)
    off_b = halo_b - (kb - 1)
    phases_a = _shift_phases(off_a, ka)
    phases_b = _shift_phases(off_b, kb)
    len_a = halo_a + tm - SUBLANES
    for j, p in enumerate(phases_a):
        sha_ref[j, 0:len_a, :] = win_ref[p:p + len_a, :]
    len_b = halo_b + tm - SUBLANES
    for j, p in enumerate(phases_b):
        shb_ref[j, 0:len_b, :] = uwin_ref[p:p + len_b, :]

    rb = CONV_CHUNK

    def tap(base_ref, shift_ref, phases, offset, r0):
        q, p = divmod(offset, SUBLANES)
        rows = pl.ds(r0 + q * SUBLANES, rb)
        return base_ref[rows, :] if p == 0 else shift_ref[phases.index(p), rows, :]

    def chunk(c, carry):
        r0 = pl.multiple_of(c * rb, rb)
        acc = jnp.zeros((rb, da), F32)
        for k in range(ka):
            acc = acc + caw_ref[k:k + 1, :] * tap(win_ref, sha_ref, phases_a, off_a + k, r0)
        y_ref[pl.ds(r0, rb), 0:da] = _ln_silu(acc, cab_ref, lg_ref, lb_ref).astype(BF16)
        accb = jnp.zeros((rb, da), F32)
        for k in range(kb):
            accb = accb + cbw_ref[k:k + 1, :] * tap(uwin_ref, shb_ref, phases_b, off_b + k, r0)
        y_ref[pl.ds(r0, rb), da:2 * da] = (bg_ref[pl.ds(r0, rb), :] * accb).astype(BF16)
        return carry

    lax.fori_loop(0, tm // rb, chunk, 0)

    x1 = x_ref[...] + _dot(y_ref[...], wout_ref[...])
    x1_ref[...] = x1
    if routed:
        h2 = _rms(x1, fg_ref[...])
        h2_ref[...] = h2
        _route(_dot(h2.astype(BF16), router_ref[...].astype(BF16)), idx_ref, gate_ref)


def _mixer_prompt(glu, u, bg, x, caw, cab, lg, lb, cbw, w_out_bf, fgain, router, layer, batch, seq):
    d = x.shape[1]
    da = glu.shape[1]
    ka, kb = caw.shape[1], cbw.shape[1]
    tm = MIX_ROWS
    nt = seq // tm
    halo_a, halo_b = 32, SUBLANES
    assert ka - 1 <= halo_a and kb - 1 <= halo_b and seq % tm == 0 and tm % halo_a == 0
    routed = router is not None
    n_rows = batch * seq

    def row(b, t):
        return (b * nt + t, 0)

    def halo(width):
        per = tm // width
        return lambda b, t: (jnp.maximum((b * nt + t) * per - 1, 0), 0)

    vec = lambda n: pl.BlockSpec((None, 1, n), lambda b, t: (layer, 0, 0))
    in_specs = [
        pl.BlockSpec((tm, da), row), pl.BlockSpec((halo_a, da), halo(halo_a)),
        pl.BlockSpec((tm, da), row), pl.BlockSpec((halo_b, da), halo(halo_b)),
        pl.BlockSpec((tm, da), row), pl.BlockSpec((tm, d), row),
        pl.BlockSpec((None, ka, da), lambda b, t: (layer, 0, 0)), vec(da), vec(da), vec(da),
        pl.BlockSpec((None, kb, da), lambda b, t: (layer, 0, 0)),
        pl.BlockSpec((None, d, d), lambda b, t: (layer, 0, 0)),
    ]
    args = [glu, glu, u, u, bg, x, caw, cab, lg, lb, cbw, w_out_bf]
    out_shape = [jax.ShapeDtypeStruct((n_rows, d), F32)]
    out_specs = [pl.BlockSpec((tm, d), row)]
    if routed:
        n_e = router.shape[-1]
        in_specs += [vec(d), pl.BlockSpec((d, n_e), lambda b, t: (0, 0))]
        args += [fgain, router]
        out_shape += [jax.ShapeDtypeStruct((n_rows, d), F32),
                      jax.ShapeDtypeStruct((n_rows, TOP_K), jnp.int32),
                      jax.ShapeDtypeStruct((n_rows, TOP_K), F32)]
        out_specs += [pl.BlockSpec((tm, d), row), pl.BlockSpec((tm, TOP_K), row),
                      pl.BlockSpec((tm, TOP_K), row)]
    n_phase_a = len(_shift_phases(halo_a - (ka - 1), ka))
    n_phase_b = len(_shift_phases(halo_b - (kb - 1), kb))
    return pl.pallas_call(
        functools.partial(_mixer_prompt_kernel, tm=tm, ka=ka, kb=kb, halo_a=halo_a, halo_b=halo_b,
                          routed=routed),
        grid=(batch, nt),
        in_specs=in_specs, out_specs=out_specs, out_shape=out_shape,
        scratch_shapes=[pltpu.VMEM((halo_a + tm, da), F32),
                        pltpu.VMEM((n_phase_a, halo_a + tm, da), F32),
                        pltpu.VMEM((halo_b + tm, da), F32),
                        pltpu.VMEM((n_phase_b, halo_b + tm, da), F32),
                        pltpu.VMEM((tm, d), BF16)],
        compiler_params=_params("arbitrary", "arbitrary"),
        name="mixer_prompt",
    )(*args)


def _mixer_sample_kernel(*refs, sb, ka, kb, n_blk, n_k, routed):
    (glu_ref, u_ref, bg_ref, x_ref, sa_ref, ss_ref, caw_ref, cab_ref, lg_ref, lb_ref, cbw_ref,
     wout_ref) = refs[:12]
    refs = refs[12:]
    if routed:
        fg_ref, router_ref = refs[:2]
        refs = refs[2:]
    nsa_ref, nss_ref, x1_ref = refs[:3]
    refs = refs[3:]
    if routed:
        h2_ref, idx_ref, gate_ref = refs[:3]
        refs = refs[3:]
    ca_ref, cb_ref, y_ref, acc_ref = refs
    s = pl.program_id(0)
    da = glu_ref.shape[-1]
    kc = wout_ref.shape[0]

    @pl.when(s < n_blk)
    def _():
        r0 = pl.multiple_of(s * sb, sb)
        w_hist = caw_ref[0:ka - 1, :]
        w_new = caw_ref[ka - 1:ka, :]
        for j in range(sb):
            new = glu_ref[pl.ds(r0 + j, 1), :]
            ca_ref[j:j + 1, :] = jnp.sum(sa_ref[j] * w_hist, axis=0, keepdims=True) + w_new * new
            nsa_ref[j, 0:ka - 2, :] = sa_ref[j, 1:ka - 1, :]
            nsa_ref[j, ka - 2:ka - 1, :] = new
            new_b = u_ref[pl.ds(r0 + j, 1), :]
            cb_ref[j:j + 1, :] = (jnp.sum(ss_ref[j] * cbw_ref[0:kb - 1, :], axis=0, keepdims=True)
                                  + cbw_ref[kb - 1:kb, :] * new_b)
            if kb > 2:
                nss_ref[j, 0:kb - 2, :] = ss_ref[j, 1:kb - 1, :]
            nss_ref[j, kb - 2:kb - 1, :] = new_b
        y_a = _ln_silu(ca_ref[...], cab_ref, lg_ref, lb_ref)
        y_b = bg_ref[pl.ds(r0, sb), :] * cb_ref[...]
        for c in range(da // kc):
            y_ref[c, pl.ds(r0, sb), :] = y_a[:, c * kc:(c + 1) * kc]
            y_ref[da // kc + c, pl.ds(r0, sb), :] = y_b[:, c * kc:(c + 1) * kc]

    @pl.when(s == n_blk)
    def _():
        acc_ref[...] = x_ref[...]

    @pl.when(s >= n_blk)
    def _():
        acc_ref[...] += _dot3(_split(y_ref[s - n_blk]), _split(wout_ref[...]))

    @pl.when(s == n_blk + n_k - 1)
    def _():
        x1 = acc_ref[...]
        x1_ref[...] = x1
        if routed:
            h2 = _rms(x1, fg_ref[...])
            h2_ref[...] = h2
            _route(_dot3(_split(h2), _split(router_ref[...])), idx_ref, gate_ref)


def _mixer_sample(glu, u, bg, x, state_a, state_s, caw, cab, lg, lb, cbw, w_out, fgain, router,
                  layer):
    n_seq, d = x.shape
    da = glu.shape[1]
    ka, kb = caw.shape[1], cbw.shape[1]
    sb, kc = SAMPLE_SEQS, SAMPLE_K
    n_blk, n_k = n_seq // sb, d // kc
    assert n_seq % sb == 0 and da % kc == 0 and d == 2 * da
    routed = router is not None
    full = lambda n: pl.BlockSpec((n_seq, n), lambda s: (0, 0))
    vec = lambda n: pl.BlockSpec((None, 1, n), lambda s: (layer, 0, 0))
    blk = lambda s: jnp.minimum(s, n_blk - 1)
    in_specs = [
        full(da), full(da), full(da), full(d),
        pl.BlockSpec((None, sb, ka - 1, da), lambda s: (layer, blk(s), 0, 0)),
        pl.BlockSpec((None, sb, kb - 1, da), lambda s: (layer, blk(s), 0, 0)),
        pl.BlockSpec((None, ka, da), lambda s: (layer, 0, 0)), vec(da), vec(da), vec(da),
        pl.BlockSpec((None, kb, da), lambda s: (layer, 0, 0)),
        pl.BlockSpec((None, kc, d), lambda s: (layer, jnp.maximum(s - n_blk, 0), 0)),
    ]
    args = [glu, u, bg, x, state_a, state_s, caw, cab, lg, lb, cbw, w_out]
    out_shape = [jax.ShapeDtypeStruct((n_seq, ka - 1, da), F32),
                 jax.ShapeDtypeStruct((n_seq, kb - 1, da), F32),
                 jax.ShapeDtypeStruct((n_seq, d), F32)]
    out_specs = [pl.BlockSpec((sb, ka - 1, da), lambda s: (blk(s), 0, 0)),
                 pl.BlockSpec((sb, kb - 1, da), lambda s: (blk(s), 0, 0)),
                 full(d)]
    if routed:
        n_e = router.shape[-1]
        in_specs += [vec(d), pl.BlockSpec((d, n_e), lambda s: (0, 0))]
        args += [fgain, router]
        out_shape += [jax.ShapeDtypeStruct((n_seq, d), F32),
                      jax.ShapeDtypeStruct((n_seq, TOP_K), jnp.int32),
                      jax.ShapeDtypeStruct((n_seq, TOP_K), F32)]
        out_specs += [full(d), full(TOP_K), full(TOP_K)]
    return pl.pallas_call(
        functools.partial(_mixer_sample_kernel, sb=sb, ka=ka, kb=kb, n_blk=n_blk, n_k=n_k,
                          routed=routed),
        grid=(n_blk + n_k,),
        in_specs=in_specs, out_specs=out_specs, out_shape=out_shape,
        scratch_shapes=[pltpu.VMEM((sb, da), F32),
                        pltpu.VMEM((sb, da), F32),
                        pltpu.VMEM((n_k, n_seq, kc), F32),
                        pltpu.VMEM((n_seq, d), F32)],
        compiler_params=_params("arbitrary"),
        name="mixer_sample",
    )(*args)


def _silu(g):
    return g * jax.nn.sigmoid(g)


def _dense_ffn_kernel(xp_ref, xs_ref, fg_ref, wg_ref, wu_ref, wd_ref, op_ref, os_ref,
                      h_ref, hs_hi_ref, hs_lo_ref, *, n_pt):
    i, f = pl.program_id(0), pl.program_id(1)

    @pl.when((f == 0) & (i < n_pt))
    def _():
        x = xp_ref[...]
        h_ref[...] = _rms(x, fg_ref[...]).astype(BF16)
        op_ref[...] = x

    @pl.when(i < n_pt)
    def _():
        h = h_ref[...]
        g = _dot(h, wg_ref[...].astype(BF16))
        up = _dot(h, wu_ref[...].astype(BF16))
        op_ref[...] += _dot((_silu(g) * up).astype(BF16), wd_ref[...].astype(BF16))

    @pl.when((f == 0) & (i == n_pt))
    def _():
        x = xs_ref[...]
        hs_hi_ref[...], hs_lo_ref[...] = _split(_rms(x, fg_ref[...]))
        os_ref[...] = x

    @pl.when(i == n_pt)
    def _():
        h = (hs_hi_ref[...], hs_lo_ref[...])
        g = _dot3(h, _split(wg_ref[...]))
        up = _dot3(h, _split(wu_ref[...]))
        os_ref[...] += _dot3(_split(_silu(g) * up), _split(wd_ref[...]))


def _dense_ffn(x_p, x_s, fgain, wg, wu, wd, layer, j):
    n_p, d = x_p.shape
    n_s = x_s.shape[0]
    ff = wg.shape[-1]
    tm, tf = DENSE_ROWS, FF_COLS
    n_pt = n_p // tm
    prow = lambda i, f: (jnp.minimum(i, n_pt - 1), 0)
    return pl.pallas_call(
        functools.partial(_dense_ffn_kernel, n_pt=n_pt),
        grid=(n_pt + 1, ff // tf),
        in_specs=[pl.BlockSpec((tm, d), prow, pipeline_mode=pl.Buffered(1)),
                  pl.BlockSpec((n_s, d), lambda i, f: (0, 0)),
                  pl.BlockSpec((None, 1, d), lambda i, f: (layer, 0, 0)),
                  pl.BlockSpec((None, d, tf), lambda i, f: (j, 0, f)),
                  pl.BlockSpec((None, d, tf), lambda i, f: (j, 0, f)),
                  pl.BlockSpec((None, tf, d), lambda i, f: (j, f, 0))],
        out_specs=[pl.BlockSpec((tm, d), prow), pl.BlockSpec((n_s, d), lambda i, f: (0, 0))],
        out_shape=[jax.ShapeDtypeStruct((n_p, d), F32), jax.ShapeDtypeStruct((n_s, d), F32)],
        scratch_shapes=[pltpu.VMEM((tm, d), BF16), pltpu.VMEM((n_s, d), BF16),
                        pltpu.VMEM((n_s, d), BF16)],
        compiler_params=_params("arbitrary", "arbitrary"),
        name="dense_ffn",
    )(x_p, x_s, fgain, wg, wu, wd)


def _expert_ffn_kernel(te_ref, tv_ref, xs_ref, sg_ref, wg_ref, wu_ref, wd_ref, o_ref):
    i, f = pl.program_id(0), pl.program_id(1)

    @pl.when(f == 0)
    def _():
        o_ref[...] = jnp.zeros_like(o_ref)

    @pl.when(tv_ref[i] == 1)
    def _():
        h = xs_ref[...]
        g = _dot(h, wg_ref[...].astype(BF16))
        up = _dot(h, wu_ref[...].astype(BF16))
        o_ref[...] += _dot((_silu(g) * up).astype(BF16), wd_ref[...].astype(BF16))

    @pl.when(f == pl.num_programs(1) - 1)
    def _():
        o_ref[...] = sg_ref[...] * o_ref[...]


def _expert_ffn(tile_expert, tile_valid, xs, slot_gate, wg, wu, wd, j):
    s, d = xs.shape
    ff = wg.shape[-1]
    tm, tf = MOE_ROWS, FF_COLS
    nf = ff // tf

    def fcol(f, tv, i):
        return f * tv[i] + (nf - 1) * (1 - tv[i])

    grid_spec = pltpu.PrefetchScalarGridSpec(
        num_scalar_prefetch=2,
        grid=(s // tm, nf),
        in_specs=[pl.BlockSpec((tm, d), lambda i, f, te, tv: (i, 0)),
                  pl.BlockSpec((tm, 1), lambda i, f, te, tv: (i, 0)),
                  pl.BlockSpec((None, None, d, tf), lambda i, f, te, tv: (j, te[i], 0, fcol(f, tv, i))),
                  pl.BlockSpec((None, None, d, tf), lambda i, f, te, tv: (j, te[i], 0, fcol(f, tv, i))),
                  pl.BlockSpec((None, None, tf, d), lambda i, f, te, tv: (j, te[i], fcol(f, tv, i), 0))],
        out_specs=pl.BlockSpec((tm, d), lambda i, f, te, tv: (i, 0)),
    )
    return pl.pallas_call(
        _expert_ffn_kernel,
        grid_spec=grid_spec,
        out_shape=jax.ShapeDtypeStruct((s, d), F32),
        compiler_params=_params("arbitrary", "arbitrary"),
        name="expert_ffn",
    )(tile_expert, tile_valid, xs, slot_gate, wg, wu, wd)


def _row_copy(src_hbm, dst_ref, src_row, dst_row, sem):
    return pltpu.make_async_copy(src_hbm.at[pl.ds(src_row, 1)], dst_ref.at[pl.ds(dst_row, 1)], sem)


def _dispatch_kernel(rid_ref, hp_hbm, hs_hbm, o_ref, buf_ref, sem):
    tg = buf_ref.shape[0]
    n_p = hp_hbm.shape[0]
    base = pl.program_id(0) * tg

    def issue(r, carry):
        tok = rid_ref[base + r]

        @pl.when(tok < n_p)
        def _():
            _row_copy(hp_hbm, buf_ref, tok, r, sem).start()

        @pl.when(tok >= n_p)
        def _():
            _row_copy(hs_hbm, buf_ref, tok - n_p, r, sem).start()

        return carry

    lax.fori_loop(0, tg, issue, 0)

    def drain(r, carry):
        _row_copy(hp_hbm, buf_ref, 0, r, sem).wait()
        return carry

    lax.fori_loop(0, tg, drain, 0)
    o_ref[...] = buf_ref[...].astype(BF16)


def _dispatch(row_ids, h2_p, h2_s):
    s = row_ids.shape[0]
    d = h2_p.shape[1]
    tg = GATHER_ROWS
    grid_spec = pltpu.PrefetchScalarGridSpec(
        num_scalar_prefetch=1,
        grid=(s // tg,),
        in_specs=[pl.BlockSpec(memory_space=pl.ANY), pl.BlockSpec(memory_space=pl.ANY)],
        out_specs=pl.BlockSpec((tg, d), lambda i, rid: (i, 0)),
        scratch_shapes=[pltpu.VMEM((tg, d), F32), pltpu.SemaphoreType.DMA(())],
    )
    return pl.pallas_call(
        _dispatch_kernel,
        grid_spec=grid_spec,
        out_shape=jax.ShapeDtypeStruct((s, d), BF16),
        compiler_params=_params("arbitrary"),
        name="dispatch",
    )(row_ids, h2_p, h2_s)


def _combine_kernel(pos_ref, xp_ref, xs_ref, g_ref, ys_hbm, op_ref, os_ref, buf_ref, sem, *,
                    n_pt, final):
    i = pl.program_id(0)
    tk = xp_ref.shape[0]
    base = i * tk

    def issue(r, carry):
        for k in range(TOP_K):
            _row_copy(ys_hbm, buf_ref.at[k], pos_ref[(base + r) * TOP_K + k], r, sem).start()
        return carry

    lax.fori_loop(0, tk, issue, 0)

    def drain(r, carry):
        for k in range(TOP_K):
            _row_copy(ys_hbm, buf_ref.at[k], 0, r, sem).wait()
        return carry

    lax.fori_loop(0, tk, drain, 0)

    def emit(x_ref, o_ref):
        x2 = x_ref[...] + (buf_ref[0] + buf_ref[1])
        o_ref[...] = _rms(x2, g_ref[...]) if final else x2

    @pl.when(i < n_pt)
    def _():
        emit(xp_ref, op_ref)

    @pl.when(i == n_pt)
    def _():
        emit(xs_ref, os_ref)


def _combine(pos, x_p, x_s, ys, final_gain, final):
    n_p, d = x_p.shape
    n_s = x_s.shape[0]
    tk = COMBINE_ROWS
    assert n_s == tk and n_p % tk == 0
    n_pt = n_p // tk
    prow = lambda i, pos: (jnp.minimum(i, n_pt - 1), 0)
    grid_spec = pltpu.PrefetchScalarGridSpec(
        num_scalar_prefetch=1,
        grid=(n_pt + 1,),
        in_specs=[pl.BlockSpec((tk, d), prow),
                  pl.BlockSpec((n_s, d), lambda i, pos: (0, 0)),
                  pl.BlockSpec((1, d), lambda i, pos: (0, 0)),
                  pl.BlockSpec(memory_space=pl.ANY)],
        out_specs=[pl.BlockSpec((tk, d), prow), pl.BlockSpec((n_s, d), lambda i, pos: (0, 0))],
        scratch_shapes=[pltpu.VMEM((TOP_K, tk, d), F32), pltpu.SemaphoreType.DMA(())],
    )
    return pl.pallas_call(
        functools.partial(_combine_kernel, n_pt=n_pt, final=final),
        grid_spec=grid_spec,
        out_shape=[jax.ShapeDtypeStruct((n_p, d), F32), jax.ShapeDtypeStruct((n_s, d), F32)],
        compiler_params=_params("arbitrary"),
        name="combine",
    )(pos, x_p, x_s, final_gain, ys)


def _final_norm_kernel(x_ref, g_ref, o_ref):
    o_ref[...] = _rms(x_ref[...], g_ref[...])


def _final_norm(x, gain):
    n, d = x.shape
    tm = min(n, IN_ROWS)
    return pl.pallas_call(
        _final_norm_kernel,
        grid=(n // tm,),
        in_specs=[pl.BlockSpec((tm, d), lambda i: (i, 0)), pl.BlockSpec((1, d), lambda i: (0, 0))],
        out_specs=pl.BlockSpec((tm, d), lambda i: (i, 0)),
        out_shape=jax.ShapeDtypeStruct((n, d), F32),
        compiler_params=_params("arbitrary"),
        name="final_norm",
    )(x, gain)


def _slot_layout(top_idx, gates, n_experts, tm):
    n = top_idx.shape[0]
    m = n * TOP_K
    n_tiles = (m + n_experts * (tm - 1)) // tm
    s = n_tiles * tm
    flat_e = top_idx.reshape(m)
    flat_g = gates.reshape(m)
    onehot = (flat_e[:, None] == jnp.arange(n_experts, dtype=jnp.int32)[None, :]).astype(jnp.int32)
    counts = jnp.sum(onehot, axis=0)
    rank = jnp.take_along_axis(jnp.cumsum(onehot, axis=0), flat_e[:, None], axis=1)[:, 0] - 1
    padded = ((counts + tm - 1) // tm) * tm
    pad_end = jnp.cumsum(padded)
    pad_start = pad_end - padded
    cnt_start = jnp.cumsum(counts) - counts
    pos = pad_start[flat_e] + rank
    tile_start = jnp.arange(n_tiles, dtype=jnp.int32) * tm
    tile_valid = (tile_start < pad_end[-1]).astype(jnp.int32)
    tile_expert = jnp.minimum(jnp.searchsorted(pad_end, tile_start, side="right"),
                              n_experts - 1).astype(jnp.int32)
    order = jnp.argsort(flat_e, stable=True).astype(jnp.int32)
    slot = jnp.arange(s, dtype=jnp.int32)
    slot_e = jnp.repeat(tile_expert, tm)
    local = slot - pad_start[slot_e]
    live = (local < counts[slot_e]) & (jnp.repeat(tile_valid, tm) == 1)
    src = order[jnp.clip(cnt_start[slot_e] + local, 0, m - 1)]
    row_ids = jnp.where(live, src // TOP_K, 0).astype(jnp.int32)
    slot_gate = jnp.where(live, flat_g[src], 0.0).astype(F32)
    return row_ids, slot_gate[:, None], pos.astype(jnp.int32), tile_expert, tile_valid


def kernel(x_prompt, x_sample, state_conv_a, state_short, mix_norm, w_in, conv_a_w, conv_a_b, ln_a_g, ln_a_b, conv_b_w, w_out, ffn_norm, dense_w_gate, dense_w_up, dense_w_down, router, exp_w_gate, exp_w_up, exp_w_down, final_norm):
    batch, seq, d = x_prompt.shape
    n_seq, dec_seq, _ = x_sample.shape
    depth = w_in.shape[0]
    da = conv_a_w.shape[-1]
    ka, kb = conv_a_w.shape[1], conv_b_w.shape[1]
    n_experts = router.shape[-1]
    assert dec_seq == 1 and conv_b_w.shape[-1] == da and w_in.shape[-1] == 5 * da
    assert seq >= ka - 1 and seq >= kb - 1
    n_p = batch * seq

    x_p = x_prompt.reshape(n_p, d)
    x_s = x_sample.reshape(n_seq, d)
    mix_gain = mix_norm.reshape(depth, 1, d)
    ffn_gain = ffn_norm.reshape(depth, 1, d)
    cab = conv_a_b.reshape(depth, 1, da)
    lg = ln_a_g.reshape(depth, 1, da)
    lb = ln_a_b.reshape(depth, 1, da)
    final_gain = final_norm.reshape(1, d)
    w_out_bf = _to_bf16(w_out)

    new_a_p, new_s_p, new_a_s, new_s_s = [], [], [], []
    normed = False
    for l in range(depth):
        glu_p, u_p, bg_p, glu_s, u_s, bg_s = _in_proj(x_p, x_s, mix_gain, w_in, l, da)
        new_a_p.append(glu_p.reshape(batch, seq, da)[:, seq - (ka - 1):])
        new_s_p.append(u_p.reshape(batch, seq, da)[:, seq - (kb - 1):])
        j = l // 2
        routed = l % 2 == 1
        rt = router[j] if routed else None
        outs_p = _mixer_prompt(glu_p, u_p, bg_p, x_p, conv_a_w, cab, lg, lb, conv_b_w, w_out_bf,
                               ffn_gain, rt, l, batch, seq)
        outs_s = _mixer_sample(glu_s, u_s, bg_s, x_s, state_conv_a, state_short, conv_a_w, cab, lg,
                               lb, conv_b_w, w_out, ffn_gain, rt, l)
        new_a_s.append(outs_s[0])
        new_s_s.append(outs_s[1])
        x1_p, x1_s = outs_p[0], outs_s[2]
        if not routed:
            x_p, x_s = _dense_ffn(x1_p, x1_s, ffn_gain, dense_w_gate, dense_w_up, dense_w_down, l, j)
        else:
            top_idx = jnp.concatenate([outs_p[2], outs_s[4]], axis=0)
            gates = jnp.concatenate([outs_p[3], outs_s[5]], axis=0)
            row_ids, slot_gate, pos, tile_expert, tile_valid = _slot_layout(
                top_idx, gates, n_experts, MOE_ROWS)
            xs = _dispatch(row_ids, outs_p[1], outs_s[3])
            ys = _expert_ffn(tile_expert, tile_valid, xs, slot_gate, exp_w_gate, exp_w_up,
                             exp_w_down, j)
            normed = l == depth - 1
            x_p, x_s = _combine(pos, x1_p, x1_s, ys, final_gain, normed)
    if not normed:
        x_p, x_s = _final_norm(x_p, final_gain), _final_norm(x_s, final_gain)
    return (x_p.reshape(batch, seq, d), x_s.reshape(n_seq, 1, d), jnp.stack(new_a_p),
            jnp.stack(new_s_p), jnp.stack(new_a_s), jnp.stack(new_s_s))
```

```python
import functools

import jax
import jax.numpy as jnp
from jax import lax
from jax.experimental import pallas as pl
from jax.experimental.pallas import tpu as pltpu

F32 = jnp.float32
BF16 = jnp.bfloat16

RMS_EPS = 1e-6
LN_EPS = 1e-5
TOP_K = 2

SUBLANES = 8
VMEM_LIMIT_BYTES = 56 * 1024 * 1024

IN_ROWS = 512
IN_COLS = 256
MIX_ROWS = 256
CONV_CHUNK = 16
SAMPLE_K = 256
DENSE_ROWS = 1024
FF_COLS = 256
MOE_ROWS = 512
MOE_FF_COLS = 512
COMBINE_ROWS = 128


def _params(*sem):
    return pltpu.CompilerParams(dimension_semantics=sem, vmem_limit_bytes=VMEM_LIMIT_BYTES)


def _rms(x, gain):
    y = x * lax.rsqrt(jnp.mean(x * x, axis=-1, keepdims=True) + RMS_EPS)
    return y * gain


def _dot(a, b):
    return jnp.dot(a, b, preferred_element_type=F32)


def _split(v):
    hi = v.astype(BF16)
    return hi, (v - hi.astype(F32)).astype(BF16)


def _dot3(a, b):
    a_hi, a_lo = a
    b_hi, b_lo = b
    return _dot(a_hi, b_hi) + (_dot(a_lo, b_hi) + _dot(a_hi, b_lo))


def _in_proj_kernel(xp_ref, xs_ref, g_ref, w0, w1, w2, w3, w4,
                    glu_p, u_p, bg_p, glu_s, u_s, bg_s, whi_ref, wlo_ref, *, tc, n_pt):
    i = pl.program_id(1)

    @pl.when(i == 0)
    def _():
        for k, w in enumerate((w0, w1, w2, w3, w4)):
            hi, lo = _split(w[...])
            whi_ref[:, k * tc:(k + 1) * tc] = hi
            wlo_ref[:, k * tc:(k + 1) * tc] = lo

    def emit(proj, glu_ref, u_ref, bg_ref):
        glu_ref[...] = proj[:, 0:tc] * jax.nn.sigmoid(proj[:, tc:2 * tc])
        bg_ref[...] = proj[:, 2 * tc:3 * tc]
        u_ref[...] = proj[:, 3 * tc:4 * tc] * proj[:, 4 * tc:5 * tc]

    @pl.when(i < n_pt)
    def _():
        h = _rms(xp_ref[...], g_ref[...]).astype(BF16)
        emit(_dot(h, whi_ref[...]), glu_p, u_p, bg_p)

    @pl.when(i == n_pt)
    def _():
        h = _split(_rms(xs_ref[...], g_ref[...]))
        emit(_dot3(h, (whi_ref[...], wlo_ref[...])), glu_s, u_s, bg_s)


def _in_proj(x_p, x_s, gain, w_in, layer, d_half):
    n_p, d = x_p.shape
    n_s = x_s.shape[0]
    tm, tc = IN_ROWS, IN_COLS
    nct, n_pt = d_half // tc, n_p // tm
    w_specs = [
        pl.BlockSpec((None, d, tc), functools.partial(lambda c, i, k: (layer, 0, k * nct + c), k=k))
        for k in range(5)
    ]
    prow = lambda c, i: jnp.minimum(i, n_pt - 1)
    out_p = jax.ShapeDtypeStruct((n_p, d_half), F32)
    out_s = jax.ShapeDtypeStruct((n_s, d_half), F32)
    op_spec = pl.BlockSpec((tm, tc), lambda c, i: (prow(c, i), c))
    os_spec = pl.BlockSpec((n_s, tc), lambda c, i: (0, c))
    return pl.pallas_call(
        functools.partial(_in_proj_kernel, tc=tc, n_pt=n_pt),
        grid=(nct, n_pt + 1),
        in_specs=[pl.BlockSpec((tm, d), lambda c, i: (prow(c, i), 0)),
                  pl.BlockSpec((n_s, d), lambda c, i: (0, 0)),
                  pl.BlockSpec((None, 1, d), lambda c, i: (layer, 0, 0))] + w_specs,
        out_specs=[op_spec, op_spec, op_spec, os_spec, os_spec, os_spec],
        out_shape=[out_p, out_p, out_p, out_s, out_s, out_s],
        scratch_shapes=[pltpu.VMEM((d, 5 * tc), BF16), pltpu.VMEM((d, 5 * tc), BF16)],
        compiler_params=_params("arbitrary", "arbitrary"),
        name="in_proj",
    )(x_p, x_s, gain, w_in, w_in, w_in, w_in, w_in)


def _cast_kernel(w_ref, o_ref):
    o_ref[...] = w_ref[...].astype(o_ref.dtype)


def _to_bf16(w):
    l, r, c = w.shape
    tr = min(r, 512)
    spec = pl.BlockSpec((None, tr, c), lambda i, j: (i, j, 0))
    return pl.pallas_call(
        _cast_kernel, grid=(l, r // tr), in_specs=[spec], out_specs=spec,
        out_shape=jax.ShapeDtypeStruct(w.shape, BF16),
        compiler_params=_params("arbitrary", "arbitrary"), name="cast_bf16",
    )(w)


def _route(logits, idx_ref, gate_ref):
    n_e = logits.shape[-1]
    lane = lax.broadcasted_iota(jnp.int32, logits.shape, 1)
    m1 = jnp.max(logits, axis=-1, keepdims=True)
    i1 = jnp.min(jnp.where(logits == m1, lane, n_e), axis=-1, keepdims=True)
    rest = jnp.where(lane == i1, -jnp.inf, logits)
    m2 = jnp.max(rest, axis=-1, keepdims=True)
    i2 = jnp.min(jnp.where(rest == m2, lane, n_e), axis=-1, keepdims=True)
    e2 = jnp.exp(m2 - m1)
    den = 1.0 + e2
    col = lax.broadcasted_iota(jnp.int32, (logits.shape[0], TOP_K), 1)
    idx_ref[...] = jnp.where(col == 0, i1, i2)
    gate_ref[...] = jnp.where(col == 0, 1.0 / den, e2 / den)


def _ln_silu(conv, cb_ref, lg_ref, lb_ref):
    v = conv + cb_ref[...]
    mu = jnp.mean(v, axis=-1, keepdims=True)
    var = jnp.mean(jnp.square(v - mu), axis=-1, keepdims=True)
    z = (v - mu) * lax.rsqrt(var + LN_EPS) * lg_ref[...] + lb_ref[...]
    return z * jax.nn.sigmoid(z)


def _shift_phases(offset, taps):
    return sorted({(offset + k) % SUBLANES for k in range(taps)} - {0})


def _mixer_prompt_kernel(*refs, tm, ka, kb, halo_a, halo_b, routed):
    (glu_ref, gluh_ref, u_ref, uh_ref, bg_ref, x_ref, caw_ref, cab_ref, lg_ref, lb_ref, cbw_ref,
     wout_ref) = refs[:12]
    refs = refs[12:]
    if routed:
        fg_ref, router_ref, x1_ref, h2_ref, idx_ref, gate_ref = refs[:6]
        refs = refs[6:]
    else:
        x1_ref = refs[0]
        refs = refs[1:]
    win_ref, sha_ref, uwin_ref, shb_ref, y_ref = refs
    t = pl.program_id(1)
    da = glu_ref.shape[-1]

    first = t == 0
    win_ref[0:halo_a, :] = jnp.where(first, 0.0, gluh_ref[...])
    win_ref[halo_a:halo_a + tm, :] = glu_ref[...]
    uwin_ref[0:halo_b, :] = jnp.where(first, 0.0, uh_ref[...])
    uwin_ref[halo_b:halo_b + tm, :] = u_ref[...]

    off_a = halo_a - (ka - 1)
    off_b = halo_b - (kb - 1)
    phases_a = _shift_phases(off_a, ka)
    phases_b = _shift_phases(off_b, kb)
    len_a = halo_a + tm - SUBLANES
    for j, p in enumerate(phases_a):
        sha_ref[j, 0:len_a, :] = win_ref[p:p + len_a, :]
    len_b = halo_b + tm - SUBLANES
    for j, p in enumerate(phases_b):
        shb_ref[j, 0:len_b, :] = uwin_ref[p:p + len_b, :]

    rb = CONV_CHUNK

    def tap(base_ref, shift_ref, phases, offset, r0):
        q, p = divmod(offset, SUBLANES)
        rows = pl.ds(r0 + q * SUBLANES, rb)
        return base_ref[rows, :] if p == 0 else shift_ref[phases.index(p), rows, :]

    def chunk(c, carry):
        r0 = pl.multiple_of(c * rb, rb)
        acc = jnp.zeros((rb, da), F32)
        for k in range(ka):
            acc = acc + caw_ref[k:k + 1, :] * tap(win_ref, sha_ref, phases_a, off_a + k, r0)
        y_ref[pl.ds(r0, rb), 0:da] = _ln_silu(acc, cab_ref, lg_ref, lb_ref).astype(BF16)
        accb = jnp.zeros((rb, da), F32)
        for k in range(kb):
            accb = accb + cbw_ref[k:k + 1, :] * tap(uwin_ref, shb_ref, phases_b, off_b + k, r0)
        y_ref[pl.ds(r0, rb), da:2 * da] = (bg_ref[pl.ds(r0, rb), :] * accb).astype(BF16)
        return carry

    lax.fori_loop(0, tm // rb, chunk, 0)

    x1 = x_ref[...] + _dot(y_ref[...], wout_ref[...])
    x1_ref[...] = x1
    if routed:
        h2 = _rms(x1, fg_ref[...])
        h2_ref[...] = h2
        _route(_dot(h2.astype(BF16), router_ref[...].astype(BF16)), idx_ref, gate_ref)


def _mixer_prompt(glu, u, bg, x, caw, cab, lg, lb, cbw, w_out_bf, fgain, router, layer, batch, seq):
    d = x.shape[1]
    da = glu.shape[1]
    ka, kb = caw.shape[1], cbw.shape[1]
    tm = MIX_ROWS
    nt = seq // tm
    halo_a, halo_b = 32, SUBLANES
    assert ka - 1 <= halo_a and kb - 1 <= halo_b and seq % tm == 0 and tm % halo_a == 0
    routed = router is not None
    n_rows = batch * seq

    def row(b, t):
        return (b * nt + t, 0)

    def halo(width):
        per = tm // width
        return lambda b, t: (jnp.maximum((b * nt + t) * per - 1, 0), 0)

    vec = lambda n: pl.BlockSpec((None, 1, n), lambda b, t: (layer, 0, 0))
    in_specs = [
        pl.BlockSpec((tm, da), row), pl.BlockSpec((halo_a, da), halo(halo_a)),
        pl.BlockSpec((tm, da), row), pl.BlockSpec((halo_b, da), halo(halo_b)),
        pl.BlockSpec((tm, da), row), pl.BlockSpec((tm, d), row),
        pl.BlockSpec((None, ka, da), lambda b, t: (layer, 0, 0)), vec(da), vec(da), vec(da),
        pl.BlockSpec((None, kb, da), lambda b, t: (layer, 0, 0)),
        pl.BlockSpec((None, d, d), lambda b, t: (layer, 0, 0)),
    ]
    args = [glu, glu, u, u, bg, x, caw, cab, lg, lb, cbw, w_out_bf]
    out_shape = [jax.ShapeDtypeStruct((n_rows, d), F32)]
    out_specs = [pl.BlockSpec((tm, d), row)]
    if routed:
        n_e = router.shape[-1]
        in_specs += [vec(d), pl.BlockSpec((d, n_e), lambda b, t: (0, 0))]
        args += [fgain, router]
        out_shape += [jax.ShapeDtypeStruct((n_rows, d), F32),
                      jax.ShapeDtypeStruct((n_rows, TOP_K), jnp.int32),
                      jax.ShapeDtypeStruct((n_rows, TOP_K), F32)]
        out_specs += [pl.BlockSpec((tm, d), row), pl.BlockSpec((tm, TOP_K), row),
                      pl.BlockSpec((tm, TOP_K), row)]
    n_phase_a = len(_shift_phases(halo_a - (ka - 1), ka))
    n_phase_b = len(_shift_phases(halo_b - (kb - 1), kb))
    return pl.pallas_call(
        functools.partial(_mixer_prompt_kernel, tm=tm, ka=ka, kb=kb, halo_a=halo_a, halo_b=halo_b,
                          routed=routed),
        grid=(batch, nt),
        in_specs=in_specs, out_specs=out_specs, out_shape=out_shape,
        scratch_shapes=[pltpu.VMEM((halo_a + tm, da), F32),
                        pltpu.VMEM((n_phase_a, halo_a + tm, da), F32),
                        pltpu.VMEM((halo_b + tm, da), F32),
                        pltpu.VMEM((n_phase_b, halo_b + tm, da), F32),
                        pltpu.VMEM((tm, d), BF16)],
        compiler_params=_params("arbitrary", "arbitrary"),
        name="mixer_prompt",
    )(*args)


def _mixer_sample_kernel(*refs, ka, kb, n_blk, n_k, routed):
    (glu_ref, u_ref, bg_ref, x_ref, sa_ref, ss_ref, caw_ref, cab_ref, lg_ref, lb_ref, cbw_ref,
     wout_ref) = refs[:12]
    refs = refs[12:]
    if routed:
        fg_ref, router_ref = refs[:2]
        refs = refs[2:]
    x1_ref = refs[0]
    refs = refs[1:]
    if routed:
        h2_ref, idx_ref, gate_ref = refs[:3]
        refs = refs[3:]
    ca_ref, y_ref, acc_ref = refs
    s = pl.program_id(0)
    n_seq, kc = glu_ref.shape
    da = n_blk * kc

    @pl.when(s < n_blk)
    def _():
        conv = caw_ref[ka - 1:ka, :] * glu_ref[...]
        for k in range(ka - 1):
            conv = conv + caw_ref[k:k + 1, :] * sa_ref[k]
        ca_ref[s] = conv
        conv_b = cbw_ref[kb - 1:kb, :] * u_ref[...]
        for k in range(kb - 1):
            conv_b = conv_b + cbw_ref[k:k + 1, :] * ss_ref[k]
        y_ref[n_blk + s] = bg_ref[...] * conv_b

    @pl.when(s == n_blk - 1)
    def _():
        tiles = [ca_ref[c] + cab_ref[:, c * kc:(c + 1) * kc] for c in range(n_blk)]
        mu = sum(jnp.sum(v, axis=-1, keepdims=True) for v in tiles) / da
        var = sum(jnp.sum(jnp.square(v - mu), axis=-1, keepdims=True) for v in tiles) / da
        inv = lax.rsqrt(var + LN_EPS)
        for c in range(n_blk):
            cols = slice(c * kc, (c + 1) * kc)
            z = (tiles[c] - mu) * inv * lg_ref[:, cols] + lb_ref[:, cols]
            y_ref[c] = z * jax.nn.sigmoid(z)

    @pl.when(s == n_blk)
    def _():
        acc_ref[...] = x_ref[...]

    @pl.when(s >= n_blk)
    def _():
        acc_ref[...] += _dot3(_split(y_ref[s - n_blk]), _split(wout_ref[...]))

    @pl.when(s == n_blk + n_k - 1)
    def _():
        x1 = acc_ref[...]
        x1_ref[...] = x1
        if routed:
            h2 = _rms(x1, fg_ref[...])
            h2_ref[...] = h2
            _route(_dot3(_split(h2), _split(router_ref[...])), idx_ref, gate_ref)


def _mixer_sample(glu, u, bg, x, state_a_t, state_s_t, caw, cab, lg, lb, cbw, w_out, fgain, router,
                  layer):
    n_seq, d = x.shape
    da = glu.shape[1]
    ka, kb = caw.shape[1], cbw.shape[1]
    kc = SAMPLE_K
    n_blk, n_k = da // kc, d // kc
    assert da % kc == 0 and d == 2 * da
    routed = router is not None
    full = lambda n: pl.BlockSpec((n_seq, n), lambda s: (0, 0))
    vec = lambda n: pl.BlockSpec((None, 1, n), lambda s: (layer, 0, 0))
    blk = lambda s: jnp.minimum(s, n_blk - 1)
    tile = pl.BlockSpec((n_seq, kc), lambda s: (0, blk(s)))
    in_specs = [
        tile, tile, tile, full(d),
        pl.BlockSpec((None, ka - 1, n_seq, kc), lambda s: (layer, 0, 0, blk(s))),
        pl.BlockSpec((None, kb - 1, n_seq, kc), lambda s: (layer, 0, 0, blk(s))),
        pl.BlockSpec((None, ka, kc), lambda s: (layer, 0, blk(s))), vec(da), vec(da), vec(da),
        pl.BlockSpec((None, kb, kc), lambda s: (layer, 0, blk(s))),
        pl.BlockSpec((None, kc, d), lambda s: (layer, jnp.maximum(s - n_blk, 0), 0)),
    ]
    args = [glu, u, bg, x, state_a_t, state_s_t, caw, cab, lg, lb, cbw, w_out]
    out_shape = [jax.ShapeDtypeStruct((n_seq, d), F32)]
    out_specs = [full(d)]
    if routed:
        n_e = router.shape[-1]
        in_specs += [vec(d), pl.BlockSpec((d, n_e), lambda s: (0, 0))]
        args += [fgain, router]
        out_shape += [jax.ShapeDtypeStruct((n_seq, d), F32),
                      jax.ShapeDtypeStruct((n_seq, TOP_K), jnp.int32),
                      jax.ShapeDtypeStruct((n_seq, TOP_K), F32)]
        out_specs += [full(d), full(TOP_K), full(TOP_K)]
    return pl.pallas_call(
        functools.partial(_mixer_sample_kernel, ka=ka, kb=kb, n_blk=n_blk, n_k=n_k, routed=routed),
        grid=(n_blk + n_k,),
        in_specs=in_specs, out_specs=out_specs, out_shape=out_shape,
        scratch_shapes=[pltpu.VMEM((n_blk, n_seq, kc), F32),
                        pltpu.VMEM((n_k, n_seq, kc), F32),
                        pltpu.VMEM((n_seq, d), F32)],
        compiler_params=_params("arbitrary"),
        name="mixer_sample",
    )(*args)


def _state_update_kernel(sa_ref, ss_ref, glu_ref, u_ref, nsa_ref, nss_ref):
    for st_ref, new_ref, o_ref in ((sa_ref, glu_ref, nsa_ref), (ss_ref, u_ref, nss_ref)):
        n_hist = st_ref.shape[0]
        for k in range(n_hist - 1):
            o_ref[k] = st_ref[k + 1]
        o_ref[n_hist - 1] = new_ref[...]


def _state_update(state_a_t, state_s_t, glu_all, u_all):
    depth, n_a, n_seq, da = state_a_t.shape
    n_b = state_s_t.shape[1]
    kc = SAMPLE_K
    plane = lambda n: pl.BlockSpec((None, n, n_seq, kc), lambda l, c: (l, 0, 0, c))
    new = pl.BlockSpec((None, n_seq, kc), lambda l, c: (l, 0, c))
    return pl.pallas_call(
        _state_update_kernel,
        grid=(depth, da // kc),
        in_specs=[plane(n_a), plane(n_b), new, new],
        out_specs=[plane(n_a), plane(n_b)],
        out_shape=[jax.ShapeDtypeStruct(state_a_t.shape, F32),
                   jax.ShapeDtypeStruct(state_s_t.shape, F32)],
        compiler_params=_params("arbitrary", "arbitrary"),
        name="state_update",
    )(state_a_t, state_s_t, glu_all, u_all)


def _silu(g):
    return g * jax.nn.sigmoid(g)


def _dense_ffn_kernel(xp_ref, xs_ref, fg_ref, wg_ref, wu_ref, wd_ref, op_ref, os_ref,
                      h_ref, hs_hi_ref, hs_lo_ref, *, n_pt):
    i, f = pl.program_id(0), pl.program_id(1)

    @pl.when((f == 0) & (i < n_pt))
    def _():
        x = xp_ref[...]
        h_ref[...] = _rms(x, fg_ref[...]).astype(BF16)
        op_ref[...] = x

    @pl.when(i < n_pt)
    def _():
        h = h_ref[...]
        g = _dot(h, wg_ref[...].astype(BF16))
        up = _dot(h, wu_ref[...].astype(BF16))
        op_ref[...] += _dot((_silu(g) * up).astype(BF16), wd_ref[...].astype(BF16))

    @pl.when((f == 0) & (i == n_pt))
    def _():
        x = xs_ref[...]
        hs_hi_ref[...], hs_lo_ref[...] = _split(_rms(x, fg_ref[...]))
        os_ref[...] = x

    @pl.when(i == n_pt)
    def _():
        h = (hs_hi_ref[...], hs_lo_ref[...])
        g = _dot3(h, _split(wg_ref[...]))
        up = _dot3(h, _split(wu_ref[...]))
        os_ref[...] += _dot3(_split(_silu(g) * up), _split(wd_ref[...]))


def _dense_ffn(x_p, x_s, fgain, wg, wu, wd, layer, j):
    n_p, d = x_p.shape
    n_s = x_s.shape[0]
    ff = wg.shape[-1]
    tm, tf = DENSE_ROWS, FF_COLS
    n_pt = n_p // tm
    prow = lambda i, f: (jnp.minimum(i, n_pt - 1), 0)
    return pl.pallas_call(
        functools.partial(_dense_ffn_kernel, n_pt=n_pt),
        grid=(n_pt + 1, ff // tf),
        in_specs=[pl.BlockSpec((tm, d), prow, pipeline_mode=pl.Buffered(1)),
                  pl.BlockSpec((n_s, d), lambda i, f: (0, 0)),
                  pl.BlockSpec((None, 1, d), lambda i, f: (layer, 0, 0)),
                  pl.BlockSpec((None, d, tf), lambda i, f: (j, 0, f)),
                  pl.BlockSpec((None, d, tf), lambda i, f: (j, 0, f)),
                  pl.BlockSpec((None, tf, d), lambda i, f: (j, f, 0))],
        out_specs=[pl.BlockSpec((tm, d), prow), pl.BlockSpec((n_s, d), lambda i, f: (0, 0))],
        out_shape=[jax.ShapeDtypeStruct((n_p, d), F32), jax.ShapeDtypeStruct((n_s, d), F32)],
        scratch_shapes=[pltpu.VMEM((tm, d), BF16), pltpu.VMEM((n_s, d), BF16),
                        pltpu.VMEM((n_s, d), BF16)],
        compiler_params=_params("arbitrary", "arbitrary"),
        name="dense_ffn",
    )(x_p, x_s, fgain, wg, wu, wd)


def _row_copy(src_hbm, dst_ref, src_row, dst_row, sem):
    return pltpu.make_async_copy(src_hbm.at[pl.ds(src_row, 1)], dst_ref.at[pl.ds(dst_row, 1)], sem)


def _expert_ffn_kernel(rid_ref, te_ref, tv_ref, hp_hbm, hs_hbm, sg_ref, wg_ref, wu_ref, wd_ref,
                       o_ref, gbuf_ref, xs_ref, sems):
    i, f = pl.program_id(0), pl.program_id(1)
    n_i = pl.num_programs(0)
    tm = xs_ref.shape[0]
    n_p = hp_hbm.shape[0]

    def start_gather(tile, slot):
        def issue(r, carry):
            tok = rid_ref[tile * tm + r]

            @pl.when(tok < n_p)
            def _():
                _row_copy(hp_hbm, gbuf_ref.at[slot], tok, r, sems.at[slot]).start()

            @pl.when(tok >= n_p)
            def _():
                _row_copy(hs_hbm, gbuf_ref.at[slot], tok - n_p, r, sems.at[slot]).start()

            return carry

        lax.fori_loop(0, tm, issue, 0)

    def wait_gather(slot):
        def drain(r, carry):
            _row_copy(hp_hbm, gbuf_ref.at[slot], 0, r, sems.at[slot]).wait()
            return carry

        lax.fori_loop(0, tm, drain, 0)

    @pl.when(f == 0)
    def _():
        o_ref[...] = jnp.zeros_like(o_ref)
        slot = i % 2

        @pl.when((i == 0) & (tv_ref[0] == 1))
        def _():
            start_gather(0, 0)

        @pl.when(tv_ref[i] == 1)
        def _():
            wait_gather(slot)
            xs_ref[...] = gbuf_ref[slot].astype(BF16)

        nxt = jnp.minimum(i + 1, n_i - 1)

        @pl.when((i + 1 < n_i) & (tv_ref[nxt] == 1))
        def _():
            start_gather(i + 1, 1 - slot)

    @pl.when(tv_ref[i] == 1)
    def _():
        h = xs_ref[...]
        g = _dot(h, wg_ref[...].astype(BF16))
        up = _dot(h, wu_ref[...].astype(BF16))
        o_ref[...] += _dot((_silu(g) * up).astype(BF16), wd_ref[...].astype(BF16))

    @pl.when(f == pl.num_programs(1) - 1)
    def _():
        o_ref[...] = sg_ref[...] * o_ref[...]


def _expert_ffn(row_ids, tile_expert, tile_valid, h2_p, h2_s, slot_gate, wg, wu, wd, j):
    s = row_ids.shape[0]
    d = h2_p.shape[1]
    ff = wg.shape[-1]
    tm, tf = MOE_ROWS, MOE_FF_COLS
    nf = ff // tf

    def fcol(f, tv, i):
        return f * tv[i] + (nf - 1) * (1 - tv[i])

    grid_spec = pltpu.PrefetchScalarGridSpec(
        num_scalar_prefetch=3,
        grid=(s // tm, nf),
        in_specs=[pl.BlockSpec(memory_space=pl.ANY), pl.BlockSpec(memory_space=pl.ANY),
                  pl.BlockSpec((tm, 1), lambda i, f, rid, te, tv: (i, 0)),
                  pl.BlockSpec((None, None, d, tf),
                               lambda i, f, rid, te, tv: (j, te[i], 0, fcol(f, tv, i))),
                  pl.BlockSpec((None, None, d, tf),
                               lambda i, f, rid, te, tv: (j, te[i], 0, fcol(f, tv, i))),
                  pl.BlockSpec((None, None, tf, d),
                               lambda i, f, rid, te, tv: (j, te[i], fcol(f, tv, i), 0))],
        out_specs=pl.BlockSpec((tm, d), lambda i, f, rid, te, tv: (i, 0)),
        scratch_shapes=[pltpu.VMEM((2, tm, d), F32), pltpu.VMEM((tm, d), BF16),
                        pltpu.SemaphoreType.DMA((2,))],
    )
    return pl.pallas_call(
        _expert_ffn_kernel,
        grid_spec=grid_spec,
        out_shape=jax.ShapeDtypeStruct((s, d), F32),
        compiler_params=_params("arbitrary", "arbitrary"),
        name="expert_ffn",
    )(row_ids, tile_expert, tile_valid, h2_p, h2_s, slot_gate, wg, wu, wd)

def _combine_kernel(pos_ref, xp_ref, xs_ref, g_ref, ys_hbm, op_ref, os_ref, buf_ref, sem, *,
                    n_pt, final):
    i = pl.program_id(0)
    tk = xp_ref.shape[0]
    base = i * tk

    def issue(r, carry):
        for k in range(TOP_K):
            _row_copy(ys_hbm, buf_ref.at[k], pos_ref[(base + r) * TOP_K + k], r, sem).start()
        return carry

    lax.fori_loop(0, tk, issue, 0)

    def drain(r, carry):
        for k in range(TOP_K):
            _row_copy(ys_hbm, buf_ref.at[k], 0, r, sem).wait()
        return carry

    lax.fori_loop(0, tk, drain, 0)

    def emit(x_ref, o_ref):
        x2 = x_ref[...] + (buf_ref[0] + buf_ref[1])
        o_ref[...] = _rms(x2, g_ref[...]) if final else x2

    @pl.when(i < n_pt)
    def _():
        emit(xp_ref, op_ref)

    @pl.when(i == n_pt)
    def _():
        emit(xs_ref, os_ref)


def _combine(pos, x_p, x_s, ys, final_gain, final):
    n_p, d = x_p.shape
    n_s = x_s.shape[0]
    tk = COMBINE_ROWS
    assert n_s == tk and n_p % tk == 0
    n_pt = n_p // tk
    prow = lambda i, pos: (jnp.minimum(i, n_pt - 1), 0)
    grid_spec = pltpu.PrefetchScalarGridSpec(
        num_scalar_prefetch=1,
        grid=(n_pt + 1,),
        in_specs=[pl.BlockSpec((tk, d), prow),
                  pl.BlockSpec((n_s, d), lambda i, pos: (0, 0)),
                  pl.BlockSpec((1, d), lambda i, pos: (0, 0)),
                  pl.BlockSpec(memory_space=pl.ANY)],
        out_specs=[pl.BlockSpec((tk, d), prow), pl.BlockSpec((n_s, d), lambda i, pos: (0, 0))],
        scratch_shapes=[pltpu.VMEM((TOP_K, tk, d), F32), pltpu.SemaphoreType.DMA(())],
    )
    return pl.pallas_call(
        functools.partial(_combine_kernel, n_pt=n_pt, final=final),
        grid_spec=grid_spec,
        out_shape=[jax.ShapeDtypeStruct((n_p, d), F32), jax.ShapeDtypeStruct((n_s, d), F32)],
        compiler_params=_params("arbitrary"),
        name="combine",
    )(pos, x_p, x_s, final_gain, ys)


def _final_norm_kernel(x_ref, g_ref, o_ref):
    o_ref[...] = _rms(x_ref[...], g_ref[...])


def _final_norm(x, gain):
    n, d = x.shape
    tm = min(n, IN_ROWS)
    return pl.pallas_call(
        _final_norm_kernel,
        grid=(n // tm,),
        in_specs=[pl.BlockSpec((tm, d), lambda i: (i, 0)), pl.BlockSpec((1, d), lambda i: (0, 0))],
        out_specs=pl.BlockSpec((tm, d), lambda i: (i, 0)),
        out_shape=jax.ShapeDtypeStruct((n, d), F32),
        compiler_params=_params("arbitrary"),
        name="final_norm",
    )(x, gain)


def _slot_layout(top_idx, gates, n_experts, tm):
    n = top_idx.shape[0]
    m = n * TOP_K
    n_tiles = (m + n_experts * (tm - 1)) // tm
    s = n_tiles * tm
    flat_e = top_idx.reshape(m)
    flat_g = gates.reshape(m)
    onehot = (flat_e[:, None] == jnp.arange(n_experts, dtype=jnp.int32)[None, :]).astype(jnp.int32)
    counts = jnp.sum(onehot, axis=0)
    rank = jnp.take_along_axis(jnp.cumsum(onehot, axis=0), flat_e[:, None], axis=1)[:, 0] - 1
    padded = ((counts + tm - 1) // tm) * tm
    pad_end = jnp.cumsum(padded)
    pad_start = pad_end - padded
    cnt_start = jnp.cumsum(counts) - counts
    pos = pad_start[flat_e] + rank
    tile_start = jnp.arange(n_tiles, dtype=jnp.int32) * tm
    tile_valid = (tile_start < pad_end[-1]).astype(jnp.int32)
    tile_expert = jnp.minimum(jnp.sum((pad_end[None, :] <= tile_start[:, None]).astype(jnp.int32), axis=1),
                              n_experts - 1)
    order = jnp.argsort(flat_e, stable=True).astype(jnp.int32)
    slot = jnp.arange(s, dtype=jnp.int32)
    slot_e = jnp.repeat(tile_expert, tm)
    local = slot - pad_start[slot_e]
    live = (local < counts[slot_e]) & (jnp.repeat(tile_valid, tm) == 1)
    src = order[jnp.clip(cnt_start[slot_e] + local, 0, m - 1)]
    row_ids = jnp.where(live, src // TOP_K, 0).astype(jnp.int32)
    slot_gate = jnp.where(live, flat_g[src], 0.0).astype(F32)
    return row_ids, slot_gate[:, None], pos.astype(jnp.int32), tile_expert, tile_valid


def kernel(x_prompt, x_sample, state_conv_a, state_short, mix_norm, w_in, conv_a_w, conv_a_b, ln_a_g, ln_a_b, conv_b_w, w_out, ffn_norm, dense_w_gate, dense_w_up, dense_w_down, router, exp_w_gate, exp_w_up, exp_w_down, final_norm):
    batch, seq, d = x_prompt.shape
    n_seq, dec_seq, _ = x_sample.shape
    depth = w_in.shape[0]
    da = conv_a_w.shape[-1]
    ka, kb = conv_a_w.shape[1], conv_b_w.shape[1]
    n_experts = router.shape[-1]
    assert dec_seq == 1 and conv_b_w.shape[-1] == da and w_in.shape[-1] == 5 * da
    assert seq >= ka - 1 and seq >= kb - 1
    n_p = batch * seq

    x_p = x_prompt.reshape(n_p, d)
    x_s = x_sample.reshape(n_seq, d)
    mix_gain = mix_norm.reshape(depth, 1, d)
    ffn_gain = ffn_norm.reshape(depth, 1, d)
    cab = conv_a_b.reshape(depth, 1, da)
    lg = ln_a_g.reshape(depth, 1, da)
    lb = ln_a_b.reshape(depth, 1, da)
    final_gain = final_norm.reshape(1, d)
    w_out_bf = _to_bf16(w_out)
    state_a_t = jnp.transpose(state_conv_a, (0, 2, 1, 3))
    state_s_t = jnp.transpose(state_short, (0, 2, 1, 3))

    new_a_p, new_s_p, glu_s_all, u_s_all = [], [], [], []
    normed = False
    for l in range(depth):
        glu_p, u_p, bg_p, glu_s, u_s, bg_s = _in_proj(x_p, x_s, mix_gain, w_in, l, da)
        new_a_p.append(glu_p.reshape(batch, seq, da)[:, seq - (ka - 1):])
        new_s_p.append(u_p.reshape(batch, seq, da)[:, seq - (kb - 1):])
        j = l // 2
        routed = l % 2 == 1
        rt = router[j] if routed else None
        outs_p = _mixer_prompt(glu_p, u_p, bg_p, x_p, conv_a_w, cab, lg, lb, conv_b_w, w_out_bf,
                               ffn_gain, rt, l, batch, seq)
        outs_s = _mixer_sample(glu_s, u_s, bg_s, x_s, state_a_t, state_s_t, conv_a_w, cab, lg,
                               lb, conv_b_w, w_out, ffn_gain, rt, l)
        glu_s_all.append(glu_s)
        u_s_all.append(u_s)
        x1_p, x1_s = outs_p[0], outs_s[0]
        if not routed:
            x_p, x_s = _dense_ffn(x1_p, x1_s, ffn_gain, dense_w_gate, dense_w_up, dense_w_down, l, j)
        else:
            top_idx = jnp.concatenate([outs_p[2], outs_s[2]], axis=0)
            gates = jnp.concatenate([outs_p[3], outs_s[3]], axis=0)
            row_ids, slot_gate, pos, tile_expert, tile_valid = _slot_layout(
                top_idx, gates, n_experts, MOE_ROWS)
            ys = _expert_ffn(row_ids, tile_expert, tile_valid, outs_p[1], outs_s[1], slot_gate,
                             exp_w_gate, exp_w_up, exp_w_down, j)
            normed = l == depth - 1
            x_p, x_s = _combine(pos, x1_p, x1_s, ys, final_gain, normed)
    if not normed:
        x_p, x_s = _final_norm(x_p, final_gain), _final_norm(x_s, final_gain)
    new_a_t, new_s_t = _state_update(state_a_t, state_s_t, jnp.stack(glu_s_all), jnp.stack(u_s_all))
    return (x_p.reshape(batch, seq, d), x_s.reshape(n_seq, 1, d), jnp.stack(new_a_p),
            jnp.stack(new_s_p), jnp.transpose(new_a_t, (0, 2, 1, 3)),
            jnp.transpose(new_s_t, (0, 2, 1, 3)))
```

```python
import functools

import jax
import jax.numpy as jnp
from jax import lax
from jax.experimental import pallas as pl
from jax.experimental.pallas import tpu as pltpu

F32 = jnp.float32
BF16 = jnp.bfloat16

RMS_EPS = 1e-6
LN_EPS = 1e-5
TOP_K = 2

SUBLANES = 8
VMEM_LIMIT_BYTES = 56 * 1024 * 1024

IN_ROWS = 512
IN_COLS = 256
MIX_ROWS = 256
CONV_CHUNK = 16
SAMPLE_K = 256
DENSE_ROWS = 1024
FF_COLS = 256
MOE_ROWS = 512
MOE_SUBS = 2
MOE_FF_COLS = 256
COMBINE_ROWS = 128


def _params(*sem):
    return pltpu.CompilerParams(dimension_semantics=sem, vmem_limit_bytes=VMEM_LIMIT_BYTES)


def _rms(x, gain):
    y = x * lax.rsqrt(jnp.mean(x * x, axis=-1, keepdims=True) + RMS_EPS)
    return y * gain


def _dot(a, b):
    return jnp.dot(a, b, preferred_element_type=F32)


def _split(v):
    hi = v.astype(BF16)
    return hi, (v - hi.astype(F32)).astype(BF16)


def _dot3(a, b):
    a_hi, a_lo = a
    b_hi, b_lo = b
    return _dot(a_hi, b_hi) + (_dot(a_lo, b_hi) + _dot(a_hi, b_lo))


def _in_proj_kernel(xp_ref, xs_ref, g_ref, w0, w1, w2, w3, w4,
                    glu_p, u_p, bg_p, glu_s, u_s, bg_s, whi_ref, wlo_ref, *, tc, n_pt):
    i = pl.program_id(1)

    @pl.when(i == 0)
    def _():
        for k, w in enumerate((w0, w1, w2, w3, w4)):
            hi, lo = _split(w[...])
            whi_ref[:, k * tc:(k + 1) * tc] = hi
            wlo_ref[:, k * tc:(k + 1) * tc] = lo

    def emit(proj, glu_ref, u_ref, bg_ref):
        glu_ref[...] = proj[:, 0:tc] * jax.nn.sigmoid(proj[:, tc:2 * tc])
        bg_ref[...] = proj[:, 2 * tc:3 * tc]
        u_ref[...] = proj[:, 3 * tc:4 * tc] * proj[:, 4 * tc:5 * tc]

    @pl.when(i < n_pt)
    def _():
        h = _rms(xp_ref[...], g_ref[...]).astype(BF16)
        emit(_dot(h, whi_ref[...]), glu_p, u_p, bg_p)

    @pl.when(i == n_pt)
    def _():
        h = _split(_rms(xs_ref[...], g_ref[...]))
        emit(_dot3(h, (whi_ref[...], wlo_ref[...])), glu_s, u_s, bg_s)


def _in_proj(x_p, x_s, gain, w_in, layer, d_half):
    n_p, d = x_p.shape
    n_s = x_s.shape[0]
    tm, tc = IN_ROWS, IN_COLS
    nct, n_pt = d_half // tc, n_p // tm
    w_specs = [
        pl.BlockSpec((None, d, tc), functools.partial(lambda c, i, k: (layer, 0, k * nct + c), k=k))
        for k in range(5)
    ]
    prow = lambda c, i: jnp.minimum(i, n_pt - 1)
    out_p = jax.ShapeDtypeStruct((n_p, d_half), F32)
    out_s = jax.ShapeDtypeStruct((n_s, d_half), F32)
    op_spec = pl.BlockSpec((tm, tc), lambda c, i: (prow(c, i), c))
    os_spec = pl.BlockSpec((n_s, tc), lambda c, i: (0, c))
    return pl.pallas_call(
        functools.partial(_in_proj_kernel, tc=tc, n_pt=n_pt),
        grid=(nct, n_pt + 1),
        in_specs=[pl.BlockSpec((tm, d), lambda c, i: (prow(c, i), 0)),
                  pl.BlockSpec((n_s, d), lambda c, i: (0, 0)),
                  pl.BlockSpec((None, 1, d), lambda c, i: (layer, 0, 0))] + w_specs,
        out_specs=[op_spec, op_spec, op_spec, os_spec, os_spec, os_spec],
        out_shape=[out_p, out_p, out_p, out_s, out_s, out_s],
        scratch_shapes=[pltpu.VMEM((d, 5 * tc), BF16), pltpu.VMEM((d, 5 * tc), BF16)],
        compiler_params=_params("arbitrary", "arbitrary"),
        name="in_proj",
    )(x_p, x_s, gain, w_in, w_in, w_in, w_in, w_in)


def _cast_kernel(w_ref, o_ref):
    o_ref[...] = w_ref[...].astype(o_ref.dtype)


def _to_bf16(w):
    l, r, c = w.shape
    tr = min(r, 512)
    spec = pl.BlockSpec((None, tr, c), lambda i, j: (i, j, 0))
    return pl.pallas_call(
        _cast_kernel, grid=(l, r // tr), in_specs=[spec], out_specs=spec,
        out_shape=jax.ShapeDtypeStruct(w.shape, BF16),
        compiler_params=_params("arbitrary", "arbitrary"), name="cast_bf16",
    )(w)


def _route(logits, idx_ref, gate_ref):
    n_e = logits.shape[-1]
    lane = lax.broadcasted_iota(jnp.int32, logits.shape, 1)
    m1 = jnp.max(logits, axis=-1, keepdims=True)
    i1 = jnp.min(jnp.where(logits == m1, lane, n_e), axis=-1, keepdims=True)
    rest = jnp.where(lane == i1, -jnp.inf, logits)
    m2 = jnp.max(rest, axis=-1, keepdims=True)
    i2 = jnp.min(jnp.where(rest == m2, lane, n_e), axis=-1, keepdims=True)
    e2 = jnp.exp(m2 - m1)
    den = 1.0 + e2
    col = lax.broadcasted_iota(jnp.int32, (logits.shape[0], TOP_K), 1)
    idx_ref[...] = jnp.where(col == 0, i1, i2)
    gate_ref[...] = jnp.where(col == 0, 1.0 / den, e2 / den)


def _ln_silu(conv, cb_ref, lg_ref, lb_ref):
    v = conv + cb_ref[...]
    mu = jnp.mean(v, axis=-1, keepdims=True)
    var = jnp.mean(jnp.square(v - mu), axis=-1, keepdims=True)
    z = (v - mu) * lax.rsqrt(var + LN_EPS) * lg_ref[...] + lb_ref[...]
    return z * jax.nn.sigmoid(z)


def _shift_phases(offset, taps):
    return sorted({(offset + k) % SUBLANES for k in range(taps)} - {0})


def _mixer_prompt_kernel(*refs, tm, ka, kb, halo_a, halo_b, routed):
    (glu_ref, gluh_ref, u_ref, uh_ref, bg_ref, x_ref, caw_ref, cab_ref, lg_ref, lb_ref, cbw_ref,
     wout_ref) = refs[:12]
    refs = refs[12:]
    if routed:
        fg_ref, router_ref, x1_ref, h2_ref, idx_ref, gate_ref = refs[:6]
        refs = refs[6:]
    else:
        x1_ref = refs[0]
        refs = refs[1:]
    win_ref, sha_ref, uwin_ref, shb_ref, y_ref = refs
    t = pl.program_id(1)
    da = glu_ref.shape[-1]

    first = t == 0
    win_ref[0:halo_a, :] = jnp.where(first, 0.0, gluh_ref[...])
    win_ref[halo_a:halo_a + tm, :] = glu_ref[...]
    uwin_ref[0:halo_b, :] = jnp.where(first, 0.0, uh_ref[...])
    uwin_ref[halo_b:halo_b + tm, :] = u_ref[...]

    off_a = halo_a - (ka - 1)
    off_b = halo_b - (kb - 1)
    phases_a = _shift_phases(off_a, ka)
    phases_b = _shift_phases(off_b, kb)
    len_a = halo_a + tm - SUBLANES
    for j, p in enumerate(phases_a):
        sha_ref[j, 0:len_a, :] = win_ref[p:p + len_a, :]
    len_b = halo_b + tm - SUBLANES
    for j, p in enumerate(phases_b):
        shb_ref[j, 0:len_b, :] = uwin_ref[p:p + len_b, :]

    rb = CONV_CHUNK

    def tap(base_ref, shift_ref, phases, offset, r0):
        q, p = divmod(offset, SUBLANES)
        rows = pl.ds(r0 + q * SUBLANES, rb)
        return base_ref[rows, :] if p == 0 else shift_ref[phases.index(p), rows, :]

    def chunk(c, carry):
        r0 = pl.multiple_of(c * rb, rb)
        acc = jnp.zeros((rb, da), F32)
        for k in range(ka):
            acc = acc + caw_ref[k:k + 1, :] * tap(win_ref, sha_ref, phases_a, off_a + k, r0)
        y_ref[pl.ds(r0, rb), 0:da] = _ln_silu(acc, cab_ref, lg_ref, lb_ref).astype(BF16)
        accb = jnp.zeros((rb, da), F32)
        for k in range(kb):
            accb = accb + cbw_ref[k:k + 1, :] * tap(uwin_ref, shb_ref, phases_b, off_b + k, r0)
        y_ref[pl.ds(r0, rb), da:2 * da] = (bg_ref[pl.ds(r0, rb), :] * accb).astype(BF16)
        return carry

    lax.fori_loop(0, tm // rb, chunk, 0)

    x1 = x_ref[...] + _dot(y_ref[...], wout_ref[...])
    x1_ref[...] = x1
    if routed:
        h2 = _rms(x1, fg_ref[...])
        h2_ref[...] = h2
        _route(_dot(h2.astype(BF16), router_ref[...].astype(BF16)), idx_ref, gate_ref)


def _mixer_prompt(glu, u, bg, x, caw, cab, lg, lb, cbw, w_out_bf, fgain, router, layer, batch, seq):
    d = x.shape[1]
    da = glu.shape[1]
    ka, kb = caw.shape[1], cbw.shape[1]
    tm = MIX_ROWS
    nt = seq // tm
    halo_a, halo_b = 32, SUBLANES
    assert ka - 1 <= halo_a and kb - 1 <= halo_b and seq % tm == 0 and tm % halo_a == 0
    routed = router is not None
    n_rows = batch * seq

    def row(b, t):
        return (b * nt + t, 0)

    def halo(width):
        per = tm // width
        return lambda b, t: (jnp.maximum((b * nt + t) * per - 1, 0), 0)

    vec = lambda n: pl.BlockSpec((None, 1, n), lambda b, t: (layer, 0, 0))
    in_specs = [
        pl.BlockSpec((tm, da), row), pl.BlockSpec((halo_a, da), halo(halo_a)),
        pl.BlockSpec((tm, da), row), pl.BlockSpec((halo_b, da), halo(halo_b)),
        pl.BlockSpec((tm, da), row), pl.BlockSpec((tm, d), row),
        pl.BlockSpec((None, ka, da), lambda b, t: (layer, 0, 0)), vec(da), vec(da), vec(da),
        pl.BlockSpec((None, kb, da), lambda b, t: (layer, 0, 0)),
        pl.BlockSpec((None, d, d), lambda b, t: (layer, 0, 0)),
    ]
    args = [glu, glu, u, u, bg, x, caw, cab, lg, lb, cbw, w_out_bf]
    out_shape = [jax.ShapeDtypeStruct((n_rows, d), F32)]
    out_specs = [pl.BlockSpec((tm, d), row)]
    if routed:
        n_e = router.shape[-1]
        in_specs += [vec(d), pl.BlockSpec((d, n_e), lambda b, t: (0, 0))]
        args += [fgain, router]
        out_shape += [jax.ShapeDtypeStruct((n_rows, d), F32),
                      jax.ShapeDtypeStruct((n_rows, TOP_K), jnp.int32),
                      jax.ShapeDtypeStruct((n_rows, TOP_K), F32)]
        out_specs += [pl.BlockSpec((tm, d), row), pl.BlockSpec((tm, TOP_K), row),
                      pl.BlockSpec((tm, TOP_K), row)]
    n_phase_a = len(_shift_phases(halo_a - (ka - 1), ka))
    n_phase_b = len(_shift_phases(halo_b - (kb - 1), kb))
    return pl.pallas_call(
        functools.partial(_mixer_prompt_kernel, tm=tm, ka=ka, kb=kb, halo_a=halo_a, halo_b=halo_b,
                          routed=routed),
        grid=(batch, nt),
        in_specs=in_specs, out_specs=out_specs, out_shape=out_shape,
        scratch_shapes=[pltpu.VMEM((halo_a + tm, da), F32),
                        pltpu.VMEM((n_phase_a, halo_a + tm, da), F32),
                        pltpu.VMEM((halo_b + tm, da), F32),
                        pltpu.VMEM((n_phase_b, halo_b + tm, da), F32),
                        pltpu.VMEM((tm, d), BF16)],
        compiler_params=_params("arbitrary", "arbitrary"),
        name="mixer_prompt",
    )(*args)


def _mixer_sample_kernel(*refs, ka, kb, n_blk, n_k, routed):
    (glu_ref, u_ref, bg_ref, x_ref, sa_ref, ss_ref, caw_ref, cab_ref, lg_ref, lb_ref, cbw_ref,
     wout_ref) = refs[:12]
    refs = refs[12:]
    if routed:
        fg_ref, router_ref = refs[:2]
        refs = refs[2:]
    x1_ref = refs[0]
    refs = refs[1:]
    if routed:
        h2_ref, idx_ref, gate_ref = refs[:3]
        refs = refs[3:]
    ca_ref, y_ref, acc_ref = refs
    s = pl.program_id(0)
    n_seq, kc = glu_ref.shape
    da = n_blk * kc

    @pl.when(s < n_blk)
    def _():
        conv = caw_ref[ka - 1:ka, :] * glu_ref[...]
        for k in range(ka - 1):
            conv = conv + caw_ref[k:k + 1, :] * sa_ref[k]
        ca_ref[s] = conv
        conv_b = cbw_ref[kb - 1:kb, :] * u_ref[...]
        for k in range(kb - 1):
            conv_b = conv_b + cbw_ref[k:k + 1, :] * ss_ref[k]
        y_ref[n_blk + s] = bg_ref[...] * conv_b

    @pl.when(s == n_blk - 1)
    def _():
        tiles = [ca_ref[c] + cab_ref[:, c * kc:(c + 1) * kc] for c in range(n_blk)]
        mu = sum(jnp.sum(v, axis=-1, keepdims=True) for v in tiles) / da
        var = sum(jnp.sum(jnp.square(v - mu), axis=-1, keepdims=True) for v in tiles) / da
        inv = lax.rsqrt(var + LN_EPS)
        for c in range(n_blk):
            cols = slice(c * kc, (c + 1) * kc)
            z = (tiles[c] - mu) * inv * lg_ref[:, cols] + lb_ref[:, cols]
            y_ref[c] = z * jax.nn.sigmoid(z)

    @pl.when(s == n_blk)
    def _():
        acc_ref[...] = x_ref[...]

    @pl.when(s >= n_blk)
    def _():
        acc_ref[...] += _dot3(_split(y_ref[s - n_blk]), _split(wout_ref[...]))

    @pl.when(s == n_blk + n_k - 1)
    def _():
        x1 = acc_ref[...]
        x1_ref[...] = x1
        if routed:
            h2 = _rms(x1, fg_ref[...])
            h2_ref[...] = h2
            _route(_dot3(_split(h2), _split(router_ref[...])), idx_ref, gate_ref)


def _mixer_sample(glu, u, bg, x, state_a_t, state_s_t, caw, cab, lg, lb, cbw, w_out, fgain, router,
                  layer):
    n_seq, d = x.shape
    da = glu.shape[1]
    ka, kb = caw.shape[1], cbw.shape[1]
    kc = SAMPLE_K
    n_blk, n_k = da // kc, d // kc
    assert da % kc == 0 and d == 2 * da
    routed = router is not None
    full = lambda n: pl.BlockSpec((n_seq, n), lambda s: (0, 0))
    vec = lambda n: pl.BlockSpec((None, 1, n), lambda s: (layer, 0, 0))
    blk = lambda s: jnp.minimum(s, n_blk - 1)
    tile = pl.BlockSpec((n_seq, kc), lambda s: (0, blk(s)))
    in_specs = [
        tile, tile, tile, full(d),
        pl.BlockSpec((None, ka - 1, n_seq, kc), lambda s: (layer, 0, 0, blk(s))),
        pl.BlockSpec((None, kb - 1, n_seq, kc), lambda s: (layer, 0, 0, blk(s))),
        pl.BlockSpec((None, ka, kc), lambda s: (layer, 0, blk(s))), vec(da), vec(da), vec(da),
        pl.BlockSpec((None, kb, kc), lambda s: (layer, 0, blk(s))),
        pl.BlockSpec((None, kc, d), lambda s: (layer, jnp.maximum(s - n_blk, 0), 0)),
    ]
    args = [glu, u, bg, x, state_a_t, state_s_t, caw, cab, lg, lb, cbw, w_out]
    out_shape = [jax.ShapeDtypeStruct((n_seq, d), F32)]
    out_specs = [full(d)]
    if routed:
        n_e = router.shape[-1]
        in_specs += [vec(d), pl.BlockSpec((d, n_e), lambda s: (0, 0))]
        args += [fgain, router]
        out_shape += [jax.ShapeDtypeStruct((n_seq, d), F32),
                      jax.ShapeDtypeStruct((n_seq, TOP_K), jnp.int32),
                      jax.ShapeDtypeStruct((n_seq, TOP_K), F32)]
        out_specs += [full(d), full(TOP_K), full(TOP_K)]
    return pl.pallas_call(
        functools.partial(_mixer_sample_kernel, ka=ka, kb=kb, n_blk=n_blk, n_k=n_k, routed=routed),
        grid=(n_blk + n_k,),
        in_specs=in_specs, out_specs=out_specs, out_shape=out_shape,
        scratch_shapes=[pltpu.VMEM((n_blk, n_seq, kc), F32),
                        pltpu.VMEM((n_k, n_seq, kc), F32),
                        pltpu.VMEM((n_seq, d), F32)],
        compiler_params=_params("arbitrary"),
        name="mixer_sample",
    )(*args)


def _state_update_kernel(sa_ref, ss_ref, glu_ref, u_ref, nsa_ref, nss_ref):
    for st_ref, new_ref, o_ref in ((sa_ref, glu_ref, nsa_ref), (ss_ref, u_ref, nss_ref)):
        n_hist = st_ref.shape[0]
        for k in range(n_hist - 1):
            o_ref[k] = st_ref[k + 1]
        o_ref[n_hist - 1] = new_ref[...]


def _state_update(state_a_t, state_s_t, glu_all, u_all):
    depth, n_a, n_seq, da = state_a_t.shape
    n_b = state_s_t.shape[1]
    kc = SAMPLE_K
    plane = lambda n: pl.BlockSpec((None, n, n_seq, kc), lambda l, c: (l, 0, 0, c))
    new = pl.BlockSpec((None, n_seq, kc), lambda l, c: (l, 0, c))
    return pl.pallas_call(
        _state_update_kernel,
        grid=(depth, da // kc),
        in_specs=[plane(n_a), plane(n_b), new, new],
        out_specs=[plane(n_a), plane(n_b)],
        out_shape=[jax.ShapeDtypeStruct(state_a_t.shape, F32),
                   jax.ShapeDtypeStruct(state_s_t.shape, F32)],
        compiler_params=_params("arbitrary", "arbitrary"),
        name="state_update",
    )(state_a_t, state_s_t, glu_all, u_all)


def _silu(g):
    return g * jax.nn.sigmoid(g)


def _dense_ffn_kernel(xp_ref, xs_ref, fg_ref, wg_ref, wu_ref, wd_ref, op_ref, os_ref,
                      h_ref, hs_hi_ref, hs_lo_ref, *, n_pt):
    i, f = pl.program_id(0), pl.program_id(1)

    @pl.when((f == 0) & (i < n_pt))
    def _():
        x = xp_ref[...]
        h_ref[...] = _rms(x, fg_ref[...]).astype(BF16)
        op_ref[...] = x

    @pl.when(i < n_pt)
    def _():
        h = h_ref[...]
        g = _dot(h, wg_ref[...].astype(BF16))
        up = _dot(h, wu_ref[...].astype(BF16))
        op_ref[...] += _dot((_silu(g) * up).astype(BF16), wd_ref[...].astype(BF16))

    @pl.when((f == 0) & (i == n_pt))
    def _():
        x = xs_ref[...]
        hs_hi_ref[...], hs_lo_ref[...] = _split(_rms(x, fg_ref[...]))
        os_ref[...] = x

    @pl.when(i == n_pt)
    def _():
        h = (hs_hi_ref[...], hs_lo_ref[...])
        g = _dot3(h, _split(wg_ref[...]))
        up = _dot3(h, _split(wu_ref[...]))
        os_ref[...] += _dot3(_split(_silu(g) * up), _split(wd_ref[...]))


def _dense_ffn(x_p, x_s, fgain, wg, wu, wd, layer, j):
    n_p, d = x_p.shape
    n_s = x_s.shape[0]
    ff = wg.shape[-1]
    tm, tf = DENSE_ROWS, FF_COLS
    n_pt = n_p // tm
    prow = lambda i, f: (jnp.minimum(i, n_pt - 1), 0)
    return pl.pallas_call(
        functools.partial(_dense_ffn_kernel, n_pt=n_pt),
        grid=(n_pt + 1, ff // tf),
        in_specs=[pl.BlockSpec((tm, d), prow, pipeline_mode=pl.Buffered(1)),
                  pl.BlockSpec((n_s, d), lambda i, f: (0, 0)),
                  pl.BlockSpec((None, 1, d), lambda i, f: (layer, 0, 0)),
                  pl.BlockSpec((None, d, tf), lambda i, f: (j, 0, f)),
                  pl.BlockSpec((None, d, tf), lambda i, f: (j, 0, f)),
                  pl.BlockSpec((None, tf, d), lambda i, f: (j, f, 0))],
        out_specs=[pl.BlockSpec((tm, d), prow), pl.BlockSpec((n_s, d), lambda i, f: (0, 0))],
        out_shape=[jax.ShapeDtypeStruct((n_p, d), F32), jax.ShapeDtypeStruct((n_s, d), F32)],
        scratch_shapes=[pltpu.VMEM((tm, d), BF16), pltpu.VMEM((n_s, d), BF16),
                        pltpu.VMEM((n_s, d), BF16)],
        compiler_params=_params("arbitrary", "arbitrary"),
        name="dense_ffn",
    )(x_p, x_s, fgain, wg, wu, wd)


def _row_copy(src_hbm, dst_ref, src_row, dst_row, sem):
    return pltpu.make_async_copy(src_hbm.at[pl.ds(src_row, 1)], dst_ref.at[pl.ds(dst_row, 1)], sem)


def _expert_ffn_kernel(rid_ref, sv_ref, wt_ref, we_ref, wa_ref, wf_ref,
                       hp_hbm, hs_hbm, sg_ref, wg_ref, wu_ref, wd_ref,
                       o_ref, gbuf_ref, xs_ref, wgb_ref, wub_ref, wdb_ref, sems, *, n_tiles):
    w, f = pl.program_id(0), pl.program_id(1)
    n_sub, sub = gbuf_ref.shape[0], gbuf_ref.shape[1]
    n_p = hp_hbm.shape[0]
    tile = wt_ref[w]

    def start_gather(t, h):
        def issue(r, carry):
            tok = rid_ref[(t * n_sub + h) * sub + r]

            @pl.when(tok < n_p)
            def _():
                _row_copy(hp_hbm, gbuf_ref.at[h], tok, r, sems.at[h]).start()

            @pl.when(tok >= n_p)
            def _():
                _row_copy(hs_hbm, gbuf_ref.at[h], tok - n_p, r, sems.at[h]).start()

            return carry

        lax.fori_loop(0, sub, issue, 0)

    def wait_gather(h):
        def drain(r, carry):
            _row_copy(hp_hbm, gbuf_ref.at[h], 0, r, sems.at[h]).wait()
            return carry

        lax.fori_loop(0, sub, drain, 0)

    @pl.when((f == 0) & (wf_ref[w] == 1))
    def _():
        o_ref[...] = jnp.zeros_like(o_ref)
        for h in range(n_sub):
            @pl.when((tile == 0) & (sv_ref[h] == 1))
            def _():
                start_gather(0, h)

            @pl.when(sv_ref[tile * n_sub + h] == 1)
            def _():
                wait_gather(h)
                xs_ref[h * sub:(h + 1) * sub, :] = gbuf_ref[h].astype(BF16)

        nxt = jnp.minimum(tile + 1, n_tiles - 1)
        for h in range(n_sub):
            @pl.when((tile + 1 < n_tiles) & (sv_ref[nxt * n_sub + h] == 1))
            def _():
                start_gather(tile + 1, h)

    active = [wa_ref[w * n_sub + h] == 1 for h in range(n_sub)]

    @pl.when(functools.reduce(jnp.logical_or, active))
    def _():
        wgb_ref[...] = wg_ref[...].astype(BF16)
        wub_ref[...] = wu_ref[...].astype(BF16)
        wdb_ref[...] = wd_ref[...].astype(BF16)

    for h in range(n_sub):
        rows = slice(h * sub, (h + 1) * sub)

        @pl.when(active[h])
        def _():
            x = xs_ref[rows, :]
            g = _dot(x, wgb_ref[...])
            up = _dot(x, wub_ref[...])
            o_ref[rows, :] += _dot((_silu(g) * up).astype(BF16), wdb_ref[...])

        @pl.when(active[h] & (f == pl.num_programs(1) - 1))
        def _():
            o_ref[rows, :] = sg_ref[rows, :] * o_ref[rows, :]


def _expert_ffn(row_ids, sub_expert, sub_valid, h2_p, h2_s, slot_gate, wg, wu, wd, j):
    n_experts = wg.shape[1]
    s = row_ids.shape[0]
    d = h2_p.shape[1]
    ff = wg.shape[-1]
    sub, n_sub, tf = MOE_ROWS, MOE_SUBS, MOE_FF_COLS
    tm = sub * n_sub
    nf = ff // tf
    n_tiles = s // tm
    assert s % tm == 0
    w_tile, w_expert, w_active, w_first = _work_items(sub_expert, sub_valid, n_sub, n_experts)
    n_items = w_tile.shape[0]

    def wcol(f, w, wa):
        busy = jnp.minimum(sum(wa[w * n_sub + h] for h in range(n_sub)), 1)
        return f * busy + (nf - 1) * (1 - busy)

    grid_spec = pltpu.PrefetchScalarGridSpec(
        num_scalar_prefetch=6,
        grid=(n_items, nf),
        in_specs=[pl.BlockSpec(memory_space=pl.ANY), pl.BlockSpec(memory_space=pl.ANY),
                  pl.BlockSpec((tm, 1), lambda w, f, rid, sv, wt, we, wa, wf: (wt[w], 0)),
                  pl.BlockSpec((None, None, d, tf),
                               lambda w, f, rid, sv, wt, we, wa, wf: (j, we[w], 0, wcol(f, w, wa))),
                  pl.BlockSpec((None, None, d, tf),
                               lambda w, f, rid, sv, wt, we, wa, wf: (j, we[w], 0, wcol(f, w, wa))),
                  pl.BlockSpec((None, None, tf, d),
                               lambda w, f, rid, sv, wt, we, wa, wf: (j, we[w], wcol(f, w, wa), 0))],
        out_specs=pl.BlockSpec((tm, d), lambda w, f, rid, sv, wt, we, wa, wf: (wt[w], 0)),
        scratch_shapes=[pltpu.VMEM((n_sub, sub, d), F32), pltpu.VMEM((tm, d), BF16),
                        pltpu.VMEM((d, tf), BF16), pltpu.VMEM((d, tf), BF16),
                        pltpu.VMEM((tf, d), BF16), pltpu.SemaphoreType.DMA((n_sub,))],
    )
    return pl.pallas_call(
        functools.partial(_expert_ffn_kernel, n_tiles=n_tiles),
        grid_spec=grid_spec,
        out_shape=jax.ShapeDtypeStruct((s, d), F32),
        compiler_params=_params("arbitrary", "arbitrary"),
        name="expert_ffn",
    )(row_ids, sub_valid, w_tile, w_expert, w_active, w_first, h2_p, h2_s, slot_gate, wg, wu, wd)


def _work_items(sub_expert, sub_valid, n_sub, n_experts):
    assert n_sub == 2
    n_tiles = sub_expert.shape[0] // n_sub
    n_items = n_tiles + n_experts - 1
    e = sub_expert.reshape(n_tiles, n_sub)
    v = sub_valid.reshape(n_tiles, n_sub)
    split = ((v[:, 0] == 1) & (v[:, 1] == 1) & (e[:, 0] != e[:, 1])).astype(jnp.int32)
    base = jnp.arange(n_tiles, dtype=jnp.int32) + jnp.cumsum(split) - split
    item = jnp.arange(n_items, dtype=jnp.int32)
    tile = jnp.sum((base[None, :] <= item[:, None]).astype(jnp.int32), axis=1) - 1
    k = item - base[tile]
    sp, e0, e1, v0, v1 = split[tile], e[tile, 0], e[tile, 1], v[tile, 0], v[tile, 1]
    real = k <= sp
    second = (sp == 1) & (k == 1)
    expert = jnp.where(second, e1, e0)
    act0 = jnp.where(real & ~second, v0, 0)
    act1 = jnp.where(real, jnp.where(sp == 1, second.astype(jnp.int32), v1), 0)
    first = (k == 0).astype(jnp.int32)
    active = jnp.stack([act0, act1], axis=1).reshape(-1).astype(jnp.int32)
    return tile.astype(jnp.int32), expert.astype(jnp.int32), active, first

def _combine_kernel(pos_ref, xp_ref, xs_ref, g_ref, ys_hbm, op_ref, os_ref, buf_ref, sem, *,
                    n_pt, final):
    i = pl.program_id(0)
    tk = xp_ref.shape[0]
    base = i * tk

    def issue(r, carry):
        for k in range(TOP_K):
            _row_copy(ys_hbm, buf_ref.at[k], pos_ref[(base + r) * TOP_K + k], r, sem).start()
        return carry

    lax.fori_loop(0, tk, issue, 0)

    def drain(r, carry):
        for k in range(TOP_K):
            _row_copy(ys_hbm, buf_ref.at[k], 0, r, sem).wait()
        return carry

    lax.fori_loop(0, tk, drain, 0)

    def emit(x_ref, o_ref):
        x2 = x_ref[...] + (buf_ref[0] + buf_ref[1])
        o_ref[...] = _rms(x2, g_ref[...]) if final else x2

    @pl.when(i < n_pt)
    def _():
        emit(xp_ref, op_ref)

    @pl.when(i == n_pt)
    def _():
        emit(xs_ref, os_ref)


def _combine(pos, x_p, x_s, ys, final_gain, final):
    n_p, d = x_p.shape
    n_s = x_s.shape[0]
    tk = COMBINE_ROWS
    assert n_s == tk and n_p % tk == 0
    n_pt = n_p // tk
    prow = lambda i, pos: (jnp.minimum(i, n_pt - 1), 0)
    grid_spec = pltpu.PrefetchScalarGridSpec(
        num_scalar_prefetch=1,
        grid=(n_pt + 1,),
        in_specs=[pl.BlockSpec((tk, d), prow),
                  pl.BlockSpec((n_s, d), lambda i, pos: (0, 0)),
                  pl.BlockSpec((1, d), lambda i, pos: (0, 0)),
                  pl.BlockSpec(memory_space=pl.ANY)],
        out_specs=[pl.BlockSpec((tk, d), prow), pl.BlockSpec((n_s, d), lambda i, pos: (0, 0))],
        scratch_shapes=[pltpu.VMEM((TOP_K, tk, d), F32), pltpu.SemaphoreType.DMA(())],
    )
    return pl.pallas_call(
        functools.partial(_combine_kernel, n_pt=n_pt, final=final),
        grid_spec=grid_spec,
        out_shape=[jax.ShapeDtypeStruct((n_p, d), F32), jax.ShapeDtypeStruct((n_s, d), F32)],
        compiler_params=_params("arbitrary"),
        name="combine",
    )(pos, x_p, x_s, final_gain, ys)


def _final_norm_kernel(x_ref, g_ref, o_ref):
    o_ref[...] = _rms(x_ref[...], g_ref[...])


def _final_norm(x, gain):
    n, d = x.shape
    tm = min(n, IN_ROWS)
    return pl.pallas_call(
        _final_norm_kernel,
        grid=(n // tm,),
        in_specs=[pl.BlockSpec((tm, d), lambda i: (i, 0)), pl.BlockSpec((1, d), lambda i: (0, 0))],
        out_specs=pl.BlockSpec((tm, d), lambda i: (i, 0)),
        out_shape=jax.ShapeDtypeStruct((n, d), F32),
        compiler_params=_params("arbitrary"),
        name="final_norm",
    )(x, gain)


def _slot_layout(top_idx, gates, n_experts, tm, group):
    n = top_idx.shape[0]
    m = n * TOP_K
    n_tiles = (m + n_experts * (tm - 1)) // tm
    n_tiles = -(-n_tiles // group) * group
    s = n_tiles * tm
    flat_e = top_idx.reshape(m)
    flat_g = gates.reshape(m)
    onehot = (flat_e[:, None] == jnp.arange(n_experts, dtype=jnp.int32)[None, :]).astype(jnp.int32)
    counts = jnp.sum(onehot, axis=0)
    rank = jnp.take_along_axis(jnp.cumsum(onehot, axis=0), flat_e[:, None], axis=1)[:, 0] - 1
    padded = ((counts + tm - 1) // tm) * tm
    pad_end = jnp.cumsum(padded)
    pad_start = pad_end - padded
    cnt_start = jnp.cumsum(counts) - counts
    pos = pad_start[flat_e] + rank
    tile_start = jnp.arange(n_tiles, dtype=jnp.int32) * tm
    tile_valid = (tile_start < pad_end[-1]).astype(jnp.int32)
    tile_expert = jnp.minimum(jnp.sum((pad_end[None, :] <= tile_start[:, None]).astype(jnp.int32), axis=1),
                              n_experts - 1)
    order = jnp.argsort(flat_e, stable=True).astype(jnp.int32)
    slot = jnp.arange(s, dtype=jnp.int32)
    slot_e = jnp.repeat(tile_expert, tm)
    local = slot - pad_start[slot_e]
    live = (local < counts[slot_e]) & (jnp.repeat(tile_valid, tm) == 1)
    src = order[jnp.clip(cnt_start[slot_e] + local, 0, m - 1)]
    row_ids = jnp.where(live, src // TOP_K, 0).astype(jnp.int32)
    slot_gate = jnp.where(live, flat_g[src], 0.0).astype(F32)
    return row_ids, slot_gate[:, None], pos.astype(jnp.int32), tile_expert, tile_valid


def kernel(x_prompt, x_sample, state_conv_a, state_short, mix_norm, w_in, conv_a_w, conv_a_b, ln_a_g, ln_a_b, conv_b_w, w_out, ffn_norm, dense_w_gate, dense_w_up, dense_w_down, router, exp_w_gate, exp_w_up, exp_w_down, final_norm):
    batch, seq, d = x_prompt.shape
    n_seq, dec_seq, _ = x_sample.shape
    depth = w_in.shape[0]
    da = conv_a_w.shape[-1]
    ka, kb = conv_a_w.shape[1], conv_b_w.shape[1]
    n_experts = router.shape[-1]
    assert dec_seq == 1 and conv_b_w.shape[-1] == da and w_in.shape[-1] == 5 * da
    assert seq >= ka - 1 and seq >= kb - 1
    n_p = batch * seq

    x_p = x_prompt.reshape(n_p, d)
    x_s = x_sample.reshape(n_seq, d)
    mix_gain = mix_norm.reshape(depth, 1, d)
    ffn_gain = ffn_norm.reshape(depth, 1, d)
    cab = conv_a_b.reshape(depth, 1, da)
    lg = ln_a_g.reshape(depth, 1, da)
    lb = ln_a_b.reshape(depth, 1, da)
    final_gain = final_norm.reshape(1, d)
    w_out_bf = _to_bf16(w_out)
    state_a_t = jnp.transpose(state_conv_a, (0, 2, 1, 3))
    state_s_t = jnp.transpose(state_short, (0, 2, 1, 3))

    new_a_p, new_s_p, glu_s_all, u_s_all = [], [], [], []
    normed = False
    for l in range(depth):
        glu_p, u_p, bg_p, glu_s, u_s, bg_s = _in_proj(x_p, x_s, mix_gain, w_in, l, da)
        new_a_p.append(glu_p.reshape(batch, seq, da)[:, seq - (ka - 1):])
        new_s_p.append(u_p.reshape(batch, seq, da)[:, seq - (kb - 1):])
        j = l // 2
        routed = l % 2 == 1
        rt = router[j] if routed else None
        outs_p = _mixer_prompt(glu_p, u_p, bg_p, x_p, conv_a_w, cab, lg, lb, conv_b_w, w_out_bf,
                               ffn_gain, rt, l, batch, seq)
        outs_s = _mixer_sample(glu_s, u_s, bg_s, x_s, state_a_t, state_s_t, conv_a_w, cab, lg,
                               lb, conv_b_w, w_out, ffn_gain, rt, l)
        glu_s_all.append(glu_s)
        u_s_all.append(u_s)
        x1_p, x1_s = outs_p[0], outs_s[0]
        if not routed:
            x_p, x_s = _dense_ffn(x1_p, x1_s, ffn_gain, dense_w_gate, dense_w_up, dense_w_down, l, j)
        else:
            top_idx = jnp.concatenate([outs_p[2], outs_s[2]], axis=0)
            gates = jnp.concatenate([outs_p[3], outs_s[3]], axis=0)
            row_ids, slot_gate, pos, tile_expert, tile_valid = _slot_layout(
                top_idx, gates, n_experts, MOE_ROWS, MOE_SUBS)
            ys = _expert_ffn(row_ids, tile_expert, tile_valid, outs_p[1], outs_s[1], slot_gate,
                             exp_w_gate, exp_w_up, exp_w_down, j)
            normed = l == depth - 1
            x_p, x_s = _combine(pos, x1_p, x1_s, ys, final_gain, normed)
    if not normed:
        x_p, x_s = _final_norm(x_p, final_gain), _final_norm(x_s, final_gain)
    new_a_t, new_s_t = _state_update(state_a_t, state_s_t, jnp.stack(glu_s_all), jnp.stack(u_s_all))
    return (x_p.reshape(batch, seq, d), x_s.reshape(n_seq, 1, d), jnp.stack(new_a_p),
            jnp.stack(new_s_p), jnp.transpose(new_a_t, (0, 2, 1, 3)),
            jnp.transpose(new_s_t, (0, 2, 1, 3)))
```

```python
import functools

import jax
import jax.numpy as jnp
from jax import lax
from jax.experimental import pallas as pl
from jax.experimental.pallas import tpu as pltpu

F32 = jnp.float32
BF16 = jnp.bfloat16

RMS_EPS = 1e-6
LN_EPS = 1e-5
TOP_K = 2

SUBLANES = 8
VMEM_LIMIT_BYTES = 56 * 1024 * 1024

IN_ROWS = 512
IN_COLS = 256
MIX_ROWS = 256
CONV_CHUNK = 16
SAMPLE_K = 256
DENSE_ROWS = 1024
FF_COLS = 256
MOE_ROWS = 576
MOE_SUBS = 2
MOE_FF_COLS = 256
COMBINE_ROWS = 128


def _params(*sem):
    return pltpu.CompilerParams(dimension_semantics=sem, vmem_limit_bytes=VMEM_LIMIT_BYTES)


def _rms(x, gain):
    y = x * lax.rsqrt(jnp.mean(x * x, axis=-1, keepdims=True) + RMS_EPS)
    return y * gain


def _dot(a, b):
    return jnp.dot(a, b, preferred_element_type=F32)


def _split(v):
    hi = v.astype(BF16)
    return hi, (v - hi.astype(F32)).astype(BF16)


def _dot3(a, b):
    a_hi, a_lo = a
    b_hi, b_lo = b
    return _dot(a_hi, b_hi) + (_dot(a_lo, b_hi) + _dot(a_hi, b_lo))


def _in_proj_kernel(xp_ref, xs_ref, g_ref, w0, w1, w2, w3, w4,
                    glu_p, u_p, bg_p, glu_s, u_s, bg_s, whi_ref, wlo_ref, *, tc, n_pt):
    i = pl.program_id(1)

    @pl.when(i == 0)
    def _():
        for k, w in enumerate((w0, w1, w2, w3, w4)):
            hi, lo = _split(w[...])
            whi_ref[:, k * tc:(k + 1) * tc] = hi
            wlo_ref[:, k * tc:(k + 1) * tc] = lo

    def emit(proj, glu_ref, u_ref, bg_ref):
        glu_ref[...] = proj[:, 0:tc] * jax.nn.sigmoid(proj[:, tc:2 * tc])
        bg_ref[...] = proj[:, 2 * tc:3 * tc]
        u_ref[...] = proj[:, 3 * tc:4 * tc] * proj[:, 4 * tc:5 * tc]

    @pl.when(i < n_pt)
    def _():
        h = _rms(xp_ref[...], g_ref[...]).astype(BF16)
        emit(_dot(h, whi_ref[...]), glu_p, u_p, bg_p)

    @pl.when(i == n_pt)
    def _():
        h = _split(_rms(xs_ref[...], g_ref[...]))
        emit(_dot3(h, (whi_ref[...], wlo_ref[...])), glu_s, u_s, bg_s)


def _in_proj(x_p, x_s, gain, w_in, layer, d_half):
    n_p, d = x_p.shape
    n_s = x_s.shape[0]
    tm, tc = IN_ROWS, IN_COLS
    nct, n_pt = d_half // tc, n_p // tm
    w_specs = [
        pl.BlockSpec((None, d, tc), functools.partial(lambda c, i, k: (layer, 0, k * nct + c), k=k))
        for k in range(5)
    ]
    prow = lambda c, i: jnp.minimum(i, n_pt - 1)
    out_p = jax.ShapeDtypeStruct((n_p, d_half), F32)
    out_s = jax.ShapeDtypeStruct((n_s, d_half), F32)
    op_spec = pl.BlockSpec((tm, tc), lambda c, i: (prow(c, i), c))
    os_spec = pl.BlockSpec((n_s, tc), lambda c, i: (0, c))
    return pl.pallas_call(
        functools.partial(_in_proj_kernel, tc=tc, n_pt=n_pt),
        grid=(nct, n_pt + 1),
        in_specs=[pl.BlockSpec((tm, d), lambda c, i: (prow(c, i), 0)),
                  pl.BlockSpec((n_s, d), lambda c, i: (0, 0)),
                  pl.BlockSpec((None, 1, d), lambda c, i: (layer, 0, 0))] + w_specs,
        out_specs=[op_spec, op_spec, op_spec, os_spec, os_spec, os_spec],
        out_shape=[out_p, out_p, out_p, out_s, out_s, out_s],
        scratch_shapes=[pltpu.VMEM((d, 5 * tc), BF16), pltpu.VMEM((d, 5 * tc), BF16)],
        compiler_params=_params("arbitrary", "arbitrary"),
        name="in_proj",
    )(x_p, x_s, gain, w_in, w_in, w_in, w_in, w_in)


def _cast_kernel(w_ref, o_ref):
    o_ref[...] = w_ref[...].astype(o_ref.dtype)


def _to_bf16(w):
    l, r, c = w.shape
    tr = min(r, 512)
    spec = pl.BlockSpec((None, tr, c), lambda i, j: (i, j, 0))
    return pl.pallas_call(
        _cast_kernel, grid=(l, r // tr), in_specs=[spec], out_specs=spec,
        out_shape=jax.ShapeDtypeStruct(w.shape, BF16),
        compiler_params=_params("arbitrary", "arbitrary"), name="cast_bf16",
    )(w)


def _route(logits, idx_ref, gate_ref):
    n_e = logits.shape[-1]
    lane = lax.broadcasted_iota(jnp.int32, logits.shape, 1)
    m1 = jnp.max(logits, axis=-1, keepdims=True)
    i1 = jnp.min(jnp.where(logits == m1, lane, n_e), axis=-1, keepdims=True)
    rest = jnp.where(lane == i1, -jnp.inf, logits)
    m2 = jnp.max(rest, axis=-1, keepdims=True)
    i2 = jnp.min(jnp.where(rest == m2, lane, n_e), axis=-1, keepdims=True)
    e2 = jnp.exp(m2 - m1)
    den = 1.0 + e2
    col = lax.broadcasted_iota(jnp.int32, (logits.shape[0], TOP_K), 1)
    idx_ref[...] = jnp.where(col == 0, i1, i2)
    gate_ref[...] = jnp.where(col == 0, 1.0 / den, e2 / den)


def _ln_silu(conv, cb_ref, lg_ref, lb_ref):
    v = conv + cb_ref[...]
    mu = jnp.mean(v, axis=-1, keepdims=True)
    var = jnp.mean(jnp.square(v - mu), axis=-1, keepdims=True)
    z = (v - mu) * lax.rsqrt(var + LN_EPS) * lg_ref[...] + lb_ref[...]
    return z * jax.nn.sigmoid(z)


def _shift_phases(offset, taps):
    return sorted({(offset + k) % SUBLANES for k in range(taps)} - {0})


def _mixer_prompt_kernel(*refs, tm, ka, kb, halo_a, halo_b, routed):
    (glu_ref, gluh_ref, u_ref, uh_ref, bg_ref, x_ref, caw_ref, cab_ref, lg_ref, lb_ref, cbw_ref,
     wout_ref) = refs[:12]
    refs = refs[12:]
    if routed:
        fg_ref, router_ref, x1_ref, h2_ref, idx_ref, gate_ref = refs[:6]
        refs = refs[6:]
    else:
        x1_ref = refs[0]
        refs = refs[1:]
    win_ref, sha_ref, uwin_ref, shb_ref, y_ref = refs
    t = pl.program_id(1)
    da = glu_ref.shape[-1]

    first = t == 0
    win_ref[0:halo_a, :] = jnp.where(first, 0.0, gluh_ref[...])
    win_ref[halo_a:halo_a + tm, :] = glu_ref[...]
    uwin_ref[0:halo_b, :] = jnp.where(first, 0.0, uh_ref[...])
    uwin_ref[halo_b:halo_b + tm, :] = u_ref[...]

    off_a = halo_a - (ka - 1)
    off_b = halo_b - (kb - 1)
    phases_a = _shift_phases(off_a, ka)
    phases_b = _shift_phases(off_b, kb)
    len_a = halo_a + tm - SUBLANES
    for j, p in enumerate(phases_a):
        sha_ref[j, 0:len_a, :] = win_ref[p:p + len_a, :]
    len_b = halo_b + tm - SUBLANES
    for j, p in enumerate(phases_b):
        shb_ref[j, 0:len_b, :] = uwin_ref[p:p + len_b, :]

    rb = CONV_CHUNK

    def tap(base_ref, shift_ref, phases, offset, r0):
        q, p = divmod(offset, SUBLANES)
        rows = pl.ds(r0 + q * SUBLANES, rb)
        return base_ref[rows, :] if p == 0 else shift_ref[phases.index(p), rows, :]

    def chunk(c, carry):
        r0 = pl.multiple_of(c * rb, rb)
        acc = jnp.zeros((rb, da), F32)
        for k in range(ka):
            acc = acc + caw_ref[k:k + 1, :] * tap(win_ref, sha_ref, phases_a, off_a + k, r0)
        y_ref[pl.ds(r0, rb), 0:da] = _ln_silu(acc, cab_ref, lg_ref, lb_ref).astype(BF16)
        accb = jnp.zeros((rb, da), F32)
        for k in range(kb):
            accb = accb + cbw_ref[k:k + 1, :] * tap(uwin_ref, shb_ref, phases_b, off_b + k, r0)
        y_ref[pl.ds(r0, rb), da:2 * da] = (bg_ref[pl.ds(r0, rb), :] * accb).astype(BF16)
        return carry

    lax.fori_loop(0, tm // rb, chunk, 0)

    x1 = x_ref[...] + _dot(y_ref[...], wout_ref[...])
    x1_ref[...] = x1
    if routed:
        h2 = _rms(x1, fg_ref[...])
        h2_ref[...] = h2
        _route(_dot(h2.astype(BF16), router_ref[...].astype(BF16)), idx_ref, gate_ref)


def _mixer_prompt(glu, u, bg, x, caw, cab, lg, lb, cbw, w_out_bf, fgain, router, layer, batch, seq):
    d = x.shape[1]
    da = glu.shape[1]
    ka, kb = caw.shape[1], cbw.shape[1]
    tm = MIX_ROWS
    nt = seq // tm
    halo_a, halo_b = 32, SUBLANES
    assert ka - 1 <= halo_a and kb - 1 <= halo_b and seq % tm == 0 and tm % halo_a == 0
    routed = router is not None
    n_rows = batch * seq

    def row(b, t):
        return (b * nt + t, 0)

    def halo(width):
        per = tm // width
        return lambda b, t: (jnp.maximum((b * nt + t) * per - 1, 0), 0)

    vec = lambda n: pl.BlockSpec((None, 1, n), lambda b, t: (layer, 0, 0))
    in_specs = [
        pl.BlockSpec((tm, da), row), pl.BlockSpec((halo_a, da), halo(halo_a)),
        pl.BlockSpec((tm, da), row), pl.BlockSpec((halo_b, da), halo(halo_b)),
        pl.BlockSpec((tm, da), row), pl.BlockSpec((tm, d), row),
        pl.BlockSpec((None, ka, da), lambda b, t: (layer, 0, 0)), vec(da), vec(da), vec(da),
        pl.BlockSpec((None, kb, da), lambda b, t: (layer, 0, 0)),
        pl.BlockSpec((None, d, d), lambda b, t: (layer, 0, 0)),
    ]
    args = [glu, glu, u, u, bg, x, caw, cab, lg, lb, cbw, w_out_bf]
    out_shape = [jax.ShapeDtypeStruct((n_rows, d), F32)]
    out_specs = [pl.BlockSpec((tm, d), row)]
    if routed:
        n_e = router.shape[-1]
        in_specs += [vec(d), pl.BlockSpec((d, n_e), lambda b, t: (0, 0))]
        args += [fgain, router]
        out_shape += [jax.ShapeDtypeStruct((n_rows, d), F32),
                      jax.ShapeDtypeStruct((n_rows, TOP_K), jnp.int32),
                      jax.ShapeDtypeStruct((n_rows, TOP_K), F32)]
        out_specs += [pl.BlockSpec((tm, d), row), pl.BlockSpec((tm, TOP_K), row),
                      pl.BlockSpec((tm, TOP_K), row)]
    n_phase_a = len(_shift_phases(halo_a - (ka - 1), ka))
    n_phase_b = len(_shift_phases(halo_b - (kb - 1), kb))
    return pl.pallas_call(
        functools.partial(_mixer_prompt_kernel, tm=tm, ka=ka, kb=kb, halo_a=halo_a, halo_b=halo_b,
                          routed=routed),
        grid=(batch, nt),
        in_specs=in_specs, out_specs=out_specs, out_shape=out_shape,
        scratch_shapes=[pltpu.VMEM((halo_a + tm, da), F32),
                        pltpu.VMEM((n_phase_a, halo_a + tm, da), F32),
                        pltpu.VMEM((halo_b + tm, da), F32),
                        pltpu.VMEM((n_phase_b, halo_b + tm, da), F32),
                        pltpu.VMEM((tm, d), BF16)],
        compiler_params=_params("arbitrary", "arbitrary"),
        name="mixer_prompt",
    )(*args)


def _mixer_sample_kernel(*refs, ka, kb, n_blk, n_k, routed):
    (glu_ref, u_ref, bg_ref, x_ref, sa_ref, ss_ref, caw_ref, cab_ref, lg_ref, lb_ref, cbw_ref,
     wout_ref) = refs[:12]
    refs = refs[12:]
    if routed:
        fg_ref, router_ref = refs[:2]
        refs = refs[2:]
    x1_ref = refs[0]
    refs = refs[1:]
    if routed:
        h2_ref, idx_ref, gate_ref = refs[:3]
        refs = refs[3:]
    ca_ref, y_ref, acc_ref = refs
    s = pl.program_id(0)
    n_seq, kc = glu_ref.shape
    da = n_blk * kc

    @pl.when(s < n_blk)
    def _():
        conv = caw_ref[ka - 1:ka, :] * glu_ref[...]
        for k in range(ka - 1):
            conv = conv + caw_ref[k:k + 1, :] * sa_ref[k]
        ca_ref[s] = conv
        conv_b = cbw_ref[kb - 1:kb, :] * u_ref[...]
        for k in range(kb - 1):
            conv_b = conv_b + cbw_ref[k:k + 1, :] * ss_ref[k]
        y_ref[n_blk + s] = bg_ref[...] * conv_b

    @pl.when(s == n_blk - 1)
    def _():
        tiles = [ca_ref[c] + cab_ref[:, c * kc:(c + 1) * kc] for c in range(n_blk)]
        mu = sum(jnp.sum(v, axis=-1, keepdims=True) for v in tiles) / da
        var = sum(jnp.sum(jnp.square(v - mu), axis=-1, keepdims=True) for v in tiles) / da
        inv = lax.rsqrt(var + LN_EPS)
        for c in range(n_blk):
            cols = slice(c * kc, (c + 1) * kc)
            z = (tiles[c] - mu) * inv * lg_ref[:, cols] + lb_ref[:, cols]
            y_ref[c] = z * jax.nn.sigmoid(z)

    @pl.when(s == n_blk)
    def _():
        acc_ref[...] = x_ref[...]

    @pl.when(s >= n_blk)
    def _():
        acc_ref[...] += _dot3(_split(y_ref[s - n_blk]), _split(wout_ref[...]))

    @pl.when(s == n_blk + n_k - 1)
    def _():
        x1 = acc_ref[...]
        x1_ref[...] = x1
        if routed:
            h2 = _rms(x1, fg_ref[...])
            h2_ref[...] = h2
            _route(_dot3(_split(h2), _split(router_ref[...])), idx_ref, gate_ref)


def _mixer_sample(glu, u, bg, x, state_a_t, state_s_t, caw, cab, lg, lb, cbw, w_out, fgain, router,
                  layer):
    n_seq, d = x.shape
    da = glu.shape[1]
    ka, kb = caw.shape[1], cbw.shape[1]
    kc = SAMPLE_K
    n_blk, n_k = da // kc, d // kc
    assert da % kc == 0 and d == 2 * da
    routed = router is not None
    full = lambda n: pl.BlockSpec((n_seq, n), lambda s: (0, 0))
    vec = lambda n: pl.BlockSpec((None, 1, n), lambda s: (layer, 0, 0))
    blk = lambda s: jnp.minimum(s, n_blk - 1)
    tile = pl.BlockSpec((n_seq, kc), lambda s: (0, blk(s)))
    in_specs = [
        tile, tile, tile, full(d),
        pl.BlockSpec((None, ka - 1, n_seq, kc), lambda s: (layer, 0, 0, blk(s))),
        pl.BlockSpec((None, kb - 1, n_seq, kc), lambda s: (layer, 0, 0, blk(s))),
        pl.BlockSpec((None, ka, kc), lambda s: (layer, 0, blk(s))), vec(da), vec(da), vec(da),
        pl.BlockSpec((None, kb, kc), lambda s: (layer, 0, blk(s))),
        pl.BlockSpec((None, kc, d), lambda s: (layer, jnp.maximum(s - n_blk, 0), 0)),
    ]
    args = [glu, u, bg, x, state_a_t, state_s_t, caw, cab, lg, lb, cbw, w_out]
    out_shape = [jax.ShapeDtypeStruct((n_seq, d), F32)]
    out_specs = [full(d)]
    if routed:
        n_e = router.shape[-1]
        in_specs += [vec(d), pl.BlockSpec((d, n_e), lambda s: (0, 0))]
        args += [fgain, router]
        out_shape += [jax.ShapeDtypeStruct((n_seq, d), F32),
                      jax.ShapeDtypeStruct((n_seq, TOP_K), jnp.int32),
                      jax.ShapeDtypeStruct((n_seq, TOP_K), F32)]
        out_specs += [full(d), full(TOP_K), full(TOP_K)]
    return pl.pallas_call(
        functools.partial(_mixer_sample_kernel, ka=ka, kb=kb, n_blk=n_blk, n_k=n_k, routed=routed),
        grid=(n_blk + n_k,),
        in_specs=in_specs, out_specs=out_specs, out_shape=out_shape,
        scratch_shapes=[pltpu.VMEM((n_blk, n_seq, kc), F32),
                        pltpu.VMEM((n_k, n_seq, kc), F32),
                        pltpu.VMEM((n_seq, d), F32)],
        compiler_params=_params("arbitrary"),
        name="mixer_sample",
    )(*args)


def _state_update_kernel(sa_ref, ss_ref, glu_ref, u_ref, nsa_ref, nss_ref):
    for st_ref, new_ref, o_ref in ((sa_ref, glu_ref, nsa_ref), (ss_ref, u_ref, nss_ref)):
        n_hist = st_ref.shape[0]
        for k in range(n_hist - 1):
            o_ref[k] = st_ref[k + 1]
        o_ref[n_hist - 1] = new_ref[...]


def _state_update(state_a_t, state_s_t, glu_all, u_all):
    depth, n_a, n_seq, da = state_a_t.shape
    n_b = state_s_t.shape[1]
    kc = SAMPLE_K
    plane = lambda n: pl.BlockSpec((None, n, n_seq, kc), lambda l, c: (l, 0, 0, c))
    new = pl.BlockSpec((None, n_seq, kc), lambda l, c: (l, 0, c))
    return pl.pallas_call(
        _state_update_kernel,
        grid=(depth, da // kc),
        in_specs=[plane(n_a), plane(n_b), new, new],
        out_specs=[plane(n_a), plane(n_b)],
        out_shape=[jax.ShapeDtypeStruct(state_a_t.shape, F32),
                   jax.ShapeDtypeStruct(state_s_t.shape, F32)],
        compiler_params=_params("arbitrary", "arbitrary"),
        name="state_update",
    )(state_a_t, state_s_t, glu_all, u_all)


def _silu(g):
    return g * jax.nn.sigmoid(g)


def _dense_ffn_kernel(xp_ref, xs_ref, fg_ref, wg_ref, wu_ref, wd_ref, op_ref, os_ref,
                      h_ref, hs_hi_ref, hs_lo_ref, *, n_pt):
    i, f = pl.program_id(0), pl.program_id(1)

    @pl.when((f == 0) & (i < n_pt))
    def _():
        x = xp_ref[...]
        h_ref[...] = _rms(x, fg_ref[...]).astype(BF16)
        op_ref[...] = x

    @pl.when(i < n_pt)
    def _():
        h = h_ref[...]
        g = _dot(h, wg_ref[...].astype(BF16))
        up = _dot(h, wu_ref[...].astype(BF16))
        op_ref[...] += _dot((_silu(g) * up).astype(BF16), wd_ref[...].astype(BF16))

    @pl.when((f == 0) & (i == n_pt))
    def _():
        x = xs_ref[...]
        hs_hi_ref[...], hs_lo_ref[...] = _split(_rms(x, fg_ref[...]))
        os_ref[...] = x

    @pl.when(i == n_pt)
    def _():
        h = (hs_hi_ref[...], hs_lo_ref[...])
        g = _dot3(h, _split(wg_ref[...]))
        up = _dot3(h, _split(wu_ref[...]))
        os_ref[...] += _dot3(_split(_silu(g) * up), _split(wd_ref[...]))


def _dense_ffn(x_p, x_s, fgain, wg, wu, wd, layer, j):
    n_p, d = x_p.shape
    n_s = x_s.shape[0]
    ff = wg.shape[-1]
    tm, tf = DENSE_ROWS, FF_COLS
    n_pt = n_p // tm
    prow = lambda i, f: (jnp.minimum(i, n_pt - 1), 0)
    return pl.pallas_call(
        functools.partial(_dense_ffn_kernel, n_pt=n_pt),
        grid=(n_pt + 1, ff // tf),
        in_specs=[pl.BlockSpec((tm, d), prow, pipeline_mode=pl.Buffered(1)),
                  pl.BlockSpec((n_s, d), lambda i, f: (0, 0)),
                  pl.BlockSpec((None, 1, d), lambda i, f: (layer, 0, 0)),
                  pl.BlockSpec((None, d, tf), lambda i, f: (j, 0, f)),
                  pl.BlockSpec((None, d, tf), lambda i, f: (j, 0, f)),
                  pl.BlockSpec((None, tf, d), lambda i, f: (j, f, 0))],
        out_specs=[pl.BlockSpec((tm, d), prow), pl.BlockSpec((n_s, d), lambda i, f: (0, 0))],
        out_shape=[jax.ShapeDtypeStruct((n_p, d), F32), jax.ShapeDtypeStruct((n_s, d), F32)],
        scratch_shapes=[pltpu.VMEM((tm, d), BF16), pltpu.VMEM((n_s, d), BF16),
                        pltpu.VMEM((n_s, d), BF16)],
        compiler_params=_params("arbitrary", "arbitrary"),
        name="dense_ffn",
    )(x_p, x_s, fgain, wg, wu, wd)


def _row_copy(src_hbm, dst_ref, src_row, dst_row, sem):
    return pltpu.make_async_copy(src_hbm.at[pl.ds(src_row, 1)], dst_ref.at[pl.ds(dst_row, 1)], sem)


def _expert_ffn_kernel(rid_ref, te_ref, sa_ref, h_hbm, sg_ref, wg_ref, wu_ref, wd_ref,
                       o_ref, land_ref, xs_ref, wgb_ref, wub_ref, wdb_ref, sem, *, n_sub, chunk):
    t, f = pl.program_id(0), pl.program_id(1)
    n_t, n_f = pl.num_programs(0), pl.num_programs(1)
    tm = xs_ref.shape[0]
    sub = tm // n_sub
    n_land = land_ref.shape[0]

    def fetch(tile, r):
        tok = rid_ref[tile * tm + jnp.minimum(r, tm - 1)]
        _row_copy(h_hbm, land_ref, tok, r, sem).start()

    def wait_landed():
        pltpu.make_async_copy(h_hbm.at[pl.ds(0, n_land)], land_ref, sem).wait()

    @pl.when(f == 0)
    def _():
        @pl.when(t == 0)
        def _():
            def body(r, carry):
                fetch(0, r)
                return carry

            lax.fori_loop(0, n_land, body, 0)

        wait_landed()
        xs_ref[...] = land_ref[0:tm, :].astype(BF16)
        o_ref[...] = jnp.zeros_like(o_ref)

    nxt = jnp.minimum(t + 1, n_t - 1)

    def prefetch():
        for r in range(chunk):
            fetch(nxt, f * chunk + r)

    def half(h):
        rows = slice(h * sub, (h + 1) * sub)
        x = xs_ref[rows, :]
        g = _dot(x, wgb_ref[...])
        up = _dot(x, wub_ref[...])
        o_ref[rows, :] += _dot((_silu(g) * up).astype(BF16), wdb_ref[...])

    active = [sa_ref[t * n_sub + h] == 1 for h in range(n_sub)]

    @pl.when(active[0])
    def _():
        wgb_ref[...] = wg_ref[...].astype(BF16)
        wub_ref[...] = wu_ref[...].astype(BF16)
        wdb_ref[...] = wd_ref[...].astype(BF16)
        prefetch()
        half(0)

    @pl.when(jnp.logical_not(active[0]))
    def _():
        prefetch()

    for h in range(1, n_sub):
        pl.when(active[h])(functools.partial(half, h))

    @pl.when(f == n_f - 1)
    def _():
        for h in range(n_sub):
            rows = slice(h * sub, (h + 1) * sub)

            @pl.when(active[h])
            def _():
                o_ref[rows, :] = sg_ref[rows, :] * o_ref[rows, :]

        @pl.when(t == n_t - 1)
        def _():
            wait_landed()


def _expert_ffn(row_ids, tile_expert, sub_active, h2, slot_gate, wg, wu, wd, j):
    s = row_ids.shape[0]
    d = h2.shape[1]
    ff = wg.shape[-1]
    sub, n_sub, tf = MOE_ROWS, MOE_SUBS, MOE_FF_COLS
    tm = sub * n_sub
    nf = ff // tf
    assert s % tm == 0
    chunk = -(-tm // nf)

    def fcol(f, t, sa):
        busy = sa[t * n_sub]
        return f * busy + (nf - 1) * (1 - busy)

    grid_spec = pltpu.PrefetchScalarGridSpec(
        num_scalar_prefetch=3,
        grid=(s // tm, nf),
        in_specs=[pl.BlockSpec(memory_space=pl.ANY),
                  pl.BlockSpec((tm, 1), lambda t, f, rid, te, sa: (t, 0)),
                  pl.BlockSpec((None, None, d, tf),
                               lambda t, f, rid, te, sa: (j, te[t], 0, fcol(f, t, sa))),
                  pl.BlockSpec((None, None, d, tf),
                               lambda t, f, rid, te, sa: (j, te[t], 0, fcol(f, t, sa))),
                  pl.BlockSpec((None, None, tf, d),
                               lambda t, f, rid, te, sa: (j, te[t], fcol(f, t, sa), 0))],
        out_specs=pl.BlockSpec((tm, d), lambda t, f, rid, te, sa: (t, 0)),
        scratch_shapes=[pltpu.VMEM((chunk * nf, d), F32), pltpu.VMEM((tm, d), BF16),
                        pltpu.VMEM((d, tf), BF16), pltpu.VMEM((d, tf), BF16),
                        pltpu.VMEM((tf, d), BF16), pltpu.SemaphoreType.DMA(())],
    )
    return pl.pallas_call(
        functools.partial(_expert_ffn_kernel, n_sub=n_sub, chunk=chunk),
        grid_spec=grid_spec,
        out_shape=jax.ShapeDtypeStruct((s, d), F32),
        compiler_params=_params("arbitrary", "arbitrary"),
        name="expert_ffn",
    )(row_ids, tile_expert, sub_active, h2, slot_gate, wg, wu, wd)


def _combine_kernel(pos_ref, xp_ref, xs_ref, g_ref, ys_hbm, op_ref, os_ref, buf_ref, sem, *,
                    n_pt, final):
    i = pl.program_id(0)
    tk = xp_ref.shape[0]
    base = i * tk

    def issue(r, carry):
        for k in range(TOP_K):
            _row_copy(ys_hbm, buf_ref.at[k], pos_ref[(base + r) * TOP_K + k], r, sem).start()
        return carry

    lax.fori_loop(0, tk, issue, 0)

    def drain(r, carry):
        for k in range(TOP_K):
            _row_copy(ys_hbm, buf_ref.at[k], 0, r, sem).wait()
        return carry

    lax.fori_loop(0, tk, drain, 0)

    def emit(x_ref, o_ref):
        x2 = x_ref[...] + (buf_ref[0] + buf_ref[1])
        o_ref[...] = _rms(x2, g_ref[...]) if final else x2

    @pl.when(i < n_pt)
    def _():
        emit(xp_ref, op_ref)

    @pl.when(i == n_pt)
    def _():
        emit(xs_ref, os_ref)


def _combine(pos, x_p, x_s, ys, final_gain, final):
    n_p, d = x_p.shape
    n_s = x_s.shape[0]
    tk = COMBINE_ROWS
    assert n_s == tk and n_p % tk == 0
    n_pt = n_p // tk
    prow = lambda i, pos: (jnp.minimum(i, n_pt - 1), 0)
    grid_spec = pltpu.PrefetchScalarGridSpec(
        num_scalar_prefetch=1,
        grid=(n_pt + 1,),
        in_specs=[pl.BlockSpec((tk, d), prow),
                  pl.BlockSpec((n_s, d), lambda i, pos: (0, 0)),
                  pl.BlockSpec((1, d), lambda i, pos: (0, 0)),
                  pl.BlockSpec(memory_space=pl.ANY)],
        out_specs=[pl.BlockSpec((tk, d), prow), pl.BlockSpec((n_s, d), lambda i, pos: (0, 0))],
        scratch_shapes=[pltpu.VMEM((TOP_K, tk, d), F32), pltpu.SemaphoreType.DMA(())],
    )
    return pl.pallas_call(
        functools.partial(_combine_kernel, n_pt=n_pt, final=final),
        grid_spec=grid_spec,
        out_shape=[jax.ShapeDtypeStruct((n_p, d), F32), jax.ShapeDtypeStruct((n_s, d), F32)],
        compiler_params=_params("arbitrary"),
        name="combine",
    )(pos, x_p, x_s, final_gain, ys)


def _final_norm_kernel(x_ref, g_ref, o_ref):
    o_ref[...] = _rms(x_ref[...], g_ref[...])


def _final_norm(x, gain):
    n, d = x.shape
    tm = min(n, IN_ROWS)
    return pl.pallas_call(
        _final_norm_kernel,
        grid=(n // tm,),
        in_specs=[pl.BlockSpec((tm, d), lambda i: (i, 0)), pl.BlockSpec((1, d), lambda i: (0, 0))],
        out_specs=pl.BlockSpec((tm, d), lambda i: (i, 0)),
        out_shape=jax.ShapeDtypeStruct((n, d), F32),
        compiler_params=_params("arbitrary"),
        name="final_norm",
    )(x, gain)


def _slot_layout(top_idx, gates, n_experts, tm, n_sub):
    n = top_idx.shape[0]
    m = n * TOP_K
    n_tiles = (m + n_experts * (tm - 1)) // tm
    s = n_tiles * tm
    flat_e = top_idx.reshape(m)
    flat_g = gates.reshape(m)
    onehot = (flat_e[:, None] == jnp.arange(n_experts, dtype=jnp.int32)[None, :]).astype(jnp.int32)
    counts = jnp.sum(onehot, axis=0)
    rank = jnp.take_along_axis(jnp.cumsum(onehot, axis=0), flat_e[:, None], axis=1)[:, 0] - 1
    padded = ((counts + tm - 1) // tm) * tm
    pad_end = jnp.cumsum(padded)
    pad_start = pad_end - padded
    cnt_start = jnp.cumsum(counts) - counts
    pos = pad_start[flat_e] + rank
    tile_start = jnp.arange(n_tiles, dtype=jnp.int32) * tm
    tile_valid = (tile_start < pad_end[-1]).astype(jnp.int32)
    tile_expert = jnp.minimum(jnp.sum((pad_end[None, :] <= tile_start[:, None]).astype(jnp.int32), axis=1),
                              n_experts - 1)
    order = jnp.argsort(flat_e, stable=True).astype(jnp.int32)
    slot = jnp.arange(s, dtype=jnp.int32)
    slot_e = jnp.repeat(tile_expert, tm)
    local = slot - pad_start[slot_e]
    live = (local < counts[slot_e]) & (jnp.repeat(tile_valid, tm) == 1)
    src = order[jnp.clip(cnt_start[slot_e] + local, 0, m - 1)]
    row_ids = jnp.where(live, src // TOP_K, 0).astype(jnp.int32)
    slot_gate = jnp.where(live, flat_g[src], 0.0).astype(F32)
    sub_active = live.reshape(n_tiles * n_sub, tm // n_sub)[:, 0].astype(jnp.int32)
    return row_ids, slot_gate[:, None], pos.astype(jnp.int32), tile_expert, sub_active


def kernel(x_prompt, x_sample, state_conv_a, state_short, mix_norm, w_in, conv_a_w, conv_a_b, ln_a_g, ln_a_b, conv_b_w, w_out, ffn_norm, dense_w_gate, dense_w_up, dense_w_down, router, exp_w_gate, exp_w_up, exp_w_down, final_norm):
    batch, seq, d = x_prompt.shape
    n_seq, dec_seq, _ = x_sample.shape
    depth = w_in.shape[0]
    da = conv_a_w.shape[-1]
    ka, kb = conv_a_w.shape[1], conv_b_w.shape[1]
    n_experts = router.shape[-1]
    assert dec_seq == 1 and conv_b_w.shape[-1] == da and w_in.shape[-1] == 5 * da
    assert seq >= ka - 1 and seq >= kb - 1
    n_p = batch * seq

    x_p = x_prompt.reshape(n_p, d)
    x_s = x_sample.reshape(n_seq, d)
    mix_gain = mix_norm.reshape(depth, 1, d)
    ffn_gain = ffn_norm.reshape(depth, 1, d)
    cab = conv_a_b.reshape(depth, 1, da)
    lg = ln_a_g.reshape(depth, 1, da)
    lb = ln_a_b.reshape(depth, 1, da)
    final_gain = final_norm.reshape(1, d)
    w_out_bf = _to_bf16(w_out)
    state_a_t = jnp.transpose(state_conv_a, (0, 2, 1, 3))
    state_s_t = jnp.transpose(state_short, (0, 2, 1, 3))

    new_a_p, new_s_p, glu_s_all, u_s_all = [], [], [], []
    normed = False
    for l in range(depth):
        glu_p, u_p, bg_p, glu_s, u_s, bg_s = _in_proj(x_p, x_s, mix_gain, w_in, l, da)
        new_a_p.append(glu_p.reshape(batch, seq, da)[:, seq - (ka - 1):])
        new_s_p.append(u_p.reshape(batch, seq, da)[:, seq - (kb - 1):])
        j = l // 2
        routed = l % 2 == 1
        rt = router[j] if routed else None
        outs_p = _mixer_prompt(glu_p, u_p, bg_p, x_p, conv_a_w, cab, lg, lb, conv_b_w, w_out_bf,
                               ffn_gain, rt, l, batch, seq)
        outs_s = _mixer_sample(glu_s, u_s, bg_s, x_s, state_a_t, state_s_t, conv_a_w, cab, lg,
                               lb, conv_b_w, w_out, ffn_gain, rt, l)
        glu_s_all.append(glu_s)
        u_s_all.append(u_s)
        x1_p, x1_s = outs_p[0], outs_s[0]
        if not routed:
            x_p, x_s = _dense_ffn(x1_p, x1_s, ffn_gain, dense_w_gate, dense_w_up, dense_w_down, l, j)
        else:
            top_idx = jnp.concatenate([outs_p[2], outs_s[2]], axis=0)
            gates = jnp.concatenate([outs_p[3], outs_s[3]], axis=0)
            h2 = jnp.concatenate([outs_p[1], outs_s[1]], axis=0)
            row_ids, slot_gate, pos, tile_expert, sub_active = _slot_layout(
                top_idx, gates, n_experts, MOE_ROWS * MOE_SUBS, MOE_SUBS)
            ys = _expert_ffn(row_ids, tile_expert, sub_active, h2, slot_gate,
                             exp_w_gate, exp_w_up, exp_w_down, j)
            normed = l == depth - 1
            x_p, x_s = _combine(pos, x1_p, x1_s, ys, final_gain, normed)
    if not normed:
        x_p, x_s = _final_norm(x_p, final_gain), _final_norm(x_s, final_gain)
    new_a_t, new_s_t = _state_update(state_a_t, state_s_t, jnp.stack(glu_s_all), jnp.stack(u_s_all))
    return (x_p.reshape(batch, seq, d), x_s.reshape(n_seq, 1, d), jnp.stack(new_a_p),
            jnp.stack(new_s_p), jnp.transpose(new_a_t, (0, 2, 1, 3)),
            jnp.transpose(new_s_t, (0, 2, 1, 3)))
```

```python
import functools

import jax
import jax.numpy as jnp
from jax import lax
from jax.experimental import pallas as pl
from jax.experimental.pallas import tpu as pltpu

F32 = jnp.float32
BF16 = jnp.bfloat16

RMS_EPS = 1e-6
LN_EPS = 1e-5
TOP_K = 2

SUBLANES = 8
LANES = 128
VMEM_LIMIT_BYTES = 56 * 1024 * 1024

IN_ROWS = 512
IN_COLS = 256
MIX_ROWS = 256
CONV_STRIP_ROWS = 32
CONV_CHUNK = 64
SAMPLE_K = 256
DENSE_ROWS = 1024
FF_COLS = 256
MOE_ROWS = 576
MOE_SUBS = 2
MOE_FF_COLS = 256
COMBINE_ROWS = 128


def _params(*sem):
    return pltpu.CompilerParams(dimension_semantics=sem, vmem_limit_bytes=VMEM_LIMIT_BYTES)


def _rms(x, gain):
    y = x * lax.rsqrt(jnp.mean(x * x, axis=-1, keepdims=True) + RMS_EPS)
    return y * gain


def _dot(a, b):
    return jnp.dot(a, b, preferred_element_type=F32)


def _split(v):
    hi = v.astype(BF16)
    return hi, (v - hi.astype(F32)).astype(BF16)


def _dot3(a, b):
    a_hi, a_lo = a
    b_hi, b_lo = b
    return _dot(a_hi, b_hi) + (_dot(a_lo, b_hi) + _dot(a_hi, b_lo))


def _in_proj_kernel(xp_ref, xs_ref, g_ref, w0, w1, w2, w3, w4,
                    glu_p, u_p, bg_p, glu_s, u_s, bg_s, whi_ref, wlo_ref, *, tc, n_pt):
    i = pl.program_id(1)

    @pl.when(i == 0)
    def _():
        for k, w in enumerate((w0, w1, w2, w3, w4)):
            hi, lo = _split(w[...])
            whi_ref[:, k * tc:(k + 1) * tc] = hi
            wlo_ref[:, k * tc:(k + 1) * tc] = lo

    def emit(proj, glu_ref, u_ref, bg_ref):
        glu_ref[...] = proj[:, 0:tc] * jax.nn.sigmoid(proj[:, tc:2 * tc])
        bg_ref[...] = proj[:, 2 * tc:3 * tc]
        u_ref[...] = proj[:, 3 * tc:4 * tc] * proj[:, 4 * tc:5 * tc]

    @pl.when(i < n_pt)
    def _():
        h = _rms(xp_ref[...], g_ref[...]).astype(BF16)
        emit(_dot(h, whi_ref[...]), glu_p, u_p, bg_p)

    @pl.when(i == n_pt)
    def _():
        h = _split(_rms(xs_ref[...], g_ref[...]))
        emit(_dot3(h, (whi_ref[...], wlo_ref[...])), glu_s, u_s, bg_s)


def _in_proj(x_p, x_s, gain, w_in, layer, d_half):
    n_p, d = x_p.shape
    n_s = x_s.shape[0]
    tm, tc = IN_ROWS, IN_COLS
    nct, n_pt = d_half // tc, n_p // tm
    w_specs = [
        pl.BlockSpec((None, d, tc), functools.partial(lambda c, i, k: (layer, 0, k * nct + c), k=k))
        for k in range(5)
    ]
    prow = lambda c, i: jnp.minimum(i, n_pt - 1)
    out_p = jax.ShapeDtypeStruct((n_p, d_half), F32)
    out_s = jax.ShapeDtypeStruct((n_s, d_half), F32)
    op_spec = pl.BlockSpec((tm, tc), lambda c, i: (prow(c, i), c))
    os_spec = pl.BlockSpec((n_s, tc), lambda c, i: (0, c))
    return pl.pallas_call(
        functools.partial(_in_proj_kernel, tc=tc, n_pt=n_pt),
        grid=(nct, n_pt + 1),
        in_specs=[pl.BlockSpec((tm, d), lambda c, i: (prow(c, i), 0)),
                  pl.BlockSpec((n_s, d), lambda c, i: (0, 0)),
                  pl.BlockSpec((None, 1, d), lambda c, i: (layer, 0, 0))] + w_specs,
        out_specs=[op_spec, op_spec, op_spec, os_spec, os_spec, os_spec],
        out_shape=[out_p, out_p, out_p, out_s, out_s, out_s],
        scratch_shapes=[pltpu.VMEM((d, 5 * tc), BF16), pltpu.VMEM((d, 5 * tc), BF16)],
        compiler_params=_params("arbitrary", "arbitrary"),
        name="in_proj",
    )(x_p, x_s, gain, w_in, w_in, w_in, w_in, w_in)


def _cast_kernel(w_ref, o_ref):
    o_ref[...] = w_ref[...].astype(o_ref.dtype)


def _to_bf16(w):
    l, r, c = w.shape
    tr = min(r, 512)
    spec = pl.BlockSpec((None, tr, c), lambda i, j: (i, j, 0))
    return pl.pallas_call(
        _cast_kernel, grid=(l, r // tr), in_specs=[spec], out_specs=spec,
        out_shape=jax.ShapeDtypeStruct(w.shape, BF16),
        compiler_params=_params("arbitrary", "arbitrary"), name="cast_bf16",
    )(w)


def _route(logits, idx_ref, gate_ref):
    n_e = logits.shape[-1]
    lane = lax.broadcasted_iota(jnp.int32, logits.shape, 1)
    m1 = jnp.max(logits, axis=-1, keepdims=True)
    i1 = jnp.min(jnp.where(logits == m1, lane, n_e), axis=-1, keepdims=True)
    rest = jnp.where(lane == i1, -jnp.inf, logits)
    m2 = jnp.max(rest, axis=-1, keepdims=True)
    i2 = jnp.min(jnp.where(rest == m2, lane, n_e), axis=-1, keepdims=True)
    e2 = jnp.exp(m2 - m1)
    den = 1.0 + e2
    col = lax.broadcasted_iota(jnp.int32, (logits.shape[0], TOP_K), 1)
    idx_ref[...] = jnp.where(col == 0, i1, i2)
    gate_ref[...] = jnp.where(col == 0, 1.0 / den, e2 / den)


def _ln_silu(conv, cb_ref, lg_ref, lb_ref):
    v = conv + cb_ref[...]
    mu = jnp.mean(v, axis=-1, keepdims=True)
    var = jnp.mean(jnp.square(v - mu), axis=-1, keepdims=True)
    z = (v - mu) * lax.rsqrt(var + LN_EPS) * lg_ref[...] + lb_ref[...]
    return z * jax.nn.sigmoid(z)


def _shift_phases(offset, taps):
    return sorted({(offset + k) % SUBLANES for k in range(taps)} - {0})


def _mixer_prompt_kernel(*refs, tm, ka, kb, halo_a, halo_b, routed, n_batch):
    (glu_ref, gluh_ref, u_ref, uh_ref, bg_ref, x_ref, caw_ref, cab_ref, lg_ref, lb_ref, cbw_ref,
     wout_ref) = refs[:12]
    refs = refs[12:]
    if routed:
        fg_ref, router_ref, x1_ref, h2_ref, idx_ref, gate_ref = refs[:6]
        refs = refs[6:]
    else:
        x1_ref = refs[0]
        refs = refs[1:]
    win_ref, sha_ref, uwin_ref, shb_ref, ca_ref, y_ref = refs
    b, t = pl.program_id(0), pl.program_id(1)
    da = glu_ref.shape[-1]

    def body():
        first = t == 0
        win_ref[0:halo_a, :] = jnp.where(first, 0.0, gluh_ref[...])
        win_ref[halo_a:halo_a + tm, :] = glu_ref[...]
        uwin_ref[0:halo_b, :] = jnp.where(first, 0.0, uh_ref[...])
        uwin_ref[halo_b:halo_b + tm, :] = u_ref[...]

        off_a = halo_a - (ka - 1)
        off_b = halo_b - (kb - 1)
        phases_a = _shift_phases(off_a, ka)
        phases_b = _shift_phases(off_b, kb)
        len_a = halo_a + tm - SUBLANES
        for j, p in enumerate(phases_a):
            sha_ref[j, 0:len_a, :] = win_ref[p:p + len_a, :]
        len_b = halo_b + tm - SUBLANES
        for j, p in enumerate(phases_b):
            shb_ref[j, 0:len_b, :] = uwin_ref[p:p + len_b, :]

        def tap(base_ref, shift_ref, phases, offset, r0, n_rows, cols):
            q, p = divmod(offset, SUBLANES)
            rows = pl.ds(r0 + q * SUBLANES, n_rows)
            return base_ref[rows, cols] if p == 0 else shift_ref[phases.index(p), rows, cols]

        strip_rows = CONV_STRIP_ROWS
        for lane0 in range(0, da, LANES):
            cols = slice(lane0, lane0 + LANES)
            taps = [jnp.broadcast_to(caw_ref[k:k + 1, cols], (SUBLANES, LANES)) for k in range(ka)]

            def strip(c, carry):
                r0 = pl.multiple_of(c * strip_rows, strip_rows)
                for g in range(0, strip_rows, SUBLANES):
                    acc = taps[0] * tap(win_ref, sha_ref, phases_a, off_a, r0 + g, SUBLANES, cols)
                    for k in range(1, ka):
                        acc = acc + taps[k] * tap(win_ref, sha_ref, phases_a, off_a + k, r0 + g,
                                                  SUBLANES, cols)
                    ca_ref[pl.ds(r0 + g, SUBLANES), cols] = acc
                return carry

            lax.fori_loop(0, tm // strip_rows, strip, 0)

        rb = CONV_CHUNK
        full = slice(None)

        def chunk(c, carry):
            r0 = pl.multiple_of(c * rb, rb)
            rows = pl.ds(r0, rb)
            y_ref[rows, 0:da] = _ln_silu(ca_ref[rows, :], cab_ref, lg_ref, lb_ref).astype(BF16)
            accb = jnp.zeros((rb, da), F32)
            for k in range(kb):
                accb = accb + cbw_ref[k:k + 1, :] * tap(uwin_ref, shb_ref, phases_b, off_b + k, r0,
                                                        rb, full)
            y_ref[rows, da:2 * da] = (bg_ref[rows, :] * accb).astype(BF16)
            return carry

        lax.fori_loop(0, tm // rb, chunk, 0)

        x1 = x_ref[...] + _dot(y_ref[...], wout_ref[...])
        x1_ref[...] = x1
        if routed:
            h2 = _rms(x1, fg_ref[...])
            h2_ref[...] = h2
            _route(_dot(h2.astype(BF16), router_ref[...].astype(BF16)), idx_ref, gate_ref)

    if not routed:
        body()
    else:
        pl.when(b < n_batch)(body)

        @pl.when((b == n_batch) & (t == 0))
        def _():
            h2_ref[...] = jnp.zeros_like(h2_ref)


def _mixer_prompt(glu, u, bg, x, caw, cab, lg, lb, cbw, w_out_bf, fgain, router, layer, batch, seq):
    d = x.shape[1]
    da = glu.shape[1]
    ka, kb = caw.shape[1], cbw.shape[1]
    tm = MIX_ROWS
    nt = seq // tm
    halo_a, halo_b = 32, SUBLANES
    assert ka - 1 <= halo_a and kb - 1 <= halo_b and seq % tm == 0 and tm % halo_a == 0
    routed = router is not None
    n_rows = batch * seq
    n_blocks = batch * nt

    def blk(b, t):
        return jnp.minimum(b * nt + t, n_blocks - 1)

    def row(b, t):
        return (blk(b, t), 0)

    def halo(width):
        per = tm // width
        return lambda b, t: (jnp.maximum(blk(b, t) * per - 1, 0), 0)

    vec = lambda n: pl.BlockSpec((None, 1, n), lambda b, t: (layer, 0, 0))
    in_specs = [
        pl.BlockSpec((tm, da), row), pl.BlockSpec((halo_a, da), halo(halo_a)),
        pl.BlockSpec((tm, da), row), pl.BlockSpec((halo_b, da), halo(halo_b)),
        pl.BlockSpec((tm, da), row), pl.BlockSpec((tm, d), row),
        pl.BlockSpec((None, ka, da), lambda b, t: (layer, 0, 0)), vec(da), vec(da), vec(da),
        pl.BlockSpec((None, kb, da), lambda b, t: (layer, 0, 0)),
        pl.BlockSpec((None, d, d), lambda b, t: (layer, 0, 0)),
    ]
    args = [glu, glu, u, u, bg, x, caw, cab, lg, lb, cbw, w_out_bf]
    out_shape = [jax.ShapeDtypeStruct((n_rows, d), F32)]
    out_specs = [pl.BlockSpec((tm, d), row)]
    if routed:
        n_e = router.shape[-1]
        in_specs += [vec(d), pl.BlockSpec((d, n_e), lambda b, t: (0, 0))]
        args += [fgain, router]
        out_shape += [jax.ShapeDtypeStruct((n_rows + tm, d), F32),
                      jax.ShapeDtypeStruct((n_rows, TOP_K), jnp.int32),
                      jax.ShapeDtypeStruct((n_rows, TOP_K), F32)]
        out_specs += [pl.BlockSpec((tm, d), lambda b, t: (jnp.minimum(b * nt + t, n_blocks), 0)),
                      pl.BlockSpec((tm, TOP_K), row), pl.BlockSpec((tm, TOP_K), row)]
    n_phase_a = len(_shift_phases(halo_a - (ka - 1), ka))
    n_phase_b = len(_shift_phases(halo_b - (kb - 1), kb))
    return pl.pallas_call(
        functools.partial(_mixer_prompt_kernel, tm=tm, ka=ka, kb=kb, halo_a=halo_a, halo_b=halo_b,
                          routed=routed, n_batch=batch),
        grid=(batch + 1 if routed else batch, nt),
        in_specs=in_specs, out_specs=out_specs, out_shape=out_shape,
        scratch_shapes=[pltpu.VMEM((halo_a + tm, da), F32),
                        pltpu.VMEM((n_phase_a, halo_a + tm, da), F32),
                        pltpu.VMEM((halo_b + tm, da), F32),
                        pltpu.VMEM((n_phase_b, halo_b + tm, da), F32),
                        pltpu.VMEM((tm, da), F32),
                        pltpu.VMEM((tm, d), BF16)],
        compiler_params=_params("arbitrary", "arbitrary"),
        name="mixer_prompt",
    )(*args)


def _mixer_sample_kernel(*refs, ka, kb, n_blk, n_k, routed):
    (glu_ref, u_ref, bg_ref, x_ref, sa_ref, ss_ref, caw_ref, cab_ref, lg_ref, lb_ref, cbw_ref,
     wout_ref) = refs[:12]
    refs = refs[12:]
    if routed:
        fg_ref, router_ref, _ = refs[:3]
        refs = refs[3:]
    x1_ref = refs[0]
    refs = refs[1:]
    if routed:
        h2_ref, idx_ref, gate_ref = refs[:3]
        refs = refs[3:]
    ca_ref, y_ref, acc_ref = refs
    s = pl.program_id(0)
    n_seq, kc = glu_ref.shape
    da = n_blk * kc

    @pl.when(s < n_blk)
    def _():
        conv = caw_ref[ka - 1:ka, :] * glu_ref[...]
        for k in range(ka - 1):
            conv = conv + caw_ref[k:k + 1, :] * sa_ref[k]
        ca_ref[s] = conv
        conv_b = cbw_ref[kb - 1:kb, :] * u_ref[...]
        for k in range(kb - 1):
            conv_b = conv_b + cbw_ref[k:k + 1, :] * ss_ref[k]
        y_ref[n_blk + s] = bg_ref[...] * conv_b

    @pl.when(s == n_blk - 1)
    def _():
        tiles = [ca_ref[c] + cab_ref[:, c * kc:(c + 1) * kc] for c in range(n_blk)]
        mu = sum(jnp.sum(v, axis=-1, keepdims=True) for v in tiles) / da
        var = sum(jnp.sum(jnp.square(v - mu), axis=-1, keepdims=True) for v in tiles) / da
        inv = lax.rsqrt(var + LN_EPS)
        for c in range(n_blk):
            cols = slice(c * kc, (c + 1) * kc)
            z = (tiles[c] - mu) * inv * lg_ref[:, cols] + lb_ref[:, cols]
            y_ref[c] = z * jax.nn.sigmoid(z)

    @pl.when(s == n_blk)
    def _():
        acc_ref[...] = x_ref[...]

    @pl.when(s >= n_blk)
    def _():
        acc_ref[...] += _dot3(_split(y_ref[s - n_blk]), _split(wout_ref[...]))

    @pl.when(s == n_blk + n_k - 1)
    def _():
        x1 = acc_ref[...]
        x1_ref[...] = x1
        if routed:
            h2 = _rms(x1, fg_ref[...])
            h2_ref[...] = h2
            _route(_dot3(_split(h2), _split(router_ref[...])), idx_ref, gate_ref)


def _mixer_sample(glu, u, bg, x, state_a_t, state_s_t, caw, cab, lg, lb, cbw, w_out, fgain, router,
                  h2_all, h2_row0, layer):
    n_seq, d = x.shape
    da = glu.shape[1]
    ka, kb = caw.shape[1], cbw.shape[1]
    kc = SAMPLE_K
    n_blk, n_k = da // kc, d // kc
    assert da % kc == 0 and d == 2 * da
    routed = router is not None
    full = lambda n: pl.BlockSpec((n_seq, n), lambda s: (0, 0))
    vec = lambda n: pl.BlockSpec((None, 1, n), lambda s: (layer, 0, 0))
    blk = lambda s: jnp.minimum(s, n_blk - 1)
    tile = pl.BlockSpec((n_seq, kc), lambda s: (0, blk(s)))
    in_specs = [
        tile, tile, tile, full(d),
        pl.BlockSpec((None, ka - 1, n_seq, kc), lambda s: (layer, 0, 0, blk(s))),
        pl.BlockSpec((None, kb - 1, n_seq, kc), lambda s: (layer, 0, 0, blk(s))),
        pl.BlockSpec((None, ka, kc), lambda s: (layer, 0, blk(s))), vec(da), vec(da), vec(da),
        pl.BlockSpec((None, kb, kc), lambda s: (layer, 0, blk(s))),
        pl.BlockSpec((None, kc, d), lambda s: (layer, jnp.maximum(s - n_blk, 0), 0)),
    ]
    args = [glu, u, bg, x, state_a_t, state_s_t, caw, cab, lg, lb, cbw, w_out]
    out_shape = [jax.ShapeDtypeStruct((n_seq, d), F32)]
    out_specs = [full(d)]
    aliases = {}
    if routed:
        n_e = router.shape[-1]
        assert h2_row0 % n_seq == 0 and h2_row0 + n_seq <= h2_all.shape[0]
        in_specs += [vec(d), pl.BlockSpec((d, n_e), lambda s: (0, 0)),
                     pl.BlockSpec(memory_space=pl.ANY)]
        args += [fgain, router, h2_all]
        aliases = {len(args) - 1: 1}
        out_shape += [jax.ShapeDtypeStruct(h2_all.shape, F32),
                      jax.ShapeDtypeStruct((n_seq, TOP_K), jnp.int32),
                      jax.ShapeDtypeStruct((n_seq, TOP_K), F32)]
        out_specs += [pl.BlockSpec((n_seq, d), lambda s: (h2_row0 // n_seq, 0)),
                      full(TOP_K), full(TOP_K)]
    return pl.pallas_call(
        functools.partial(_mixer_sample_kernel, ka=ka, kb=kb, n_blk=n_blk, n_k=n_k, routed=routed),
        grid=(n_blk + n_k,),
        in_specs=in_specs, out_specs=out_specs, out_shape=out_shape,
        input_output_aliases=aliases,
        scratch_shapes=[pltpu.VMEM((n_blk, n_seq, kc), F32),
                        pltpu.VMEM((n_k, n_seq, kc), F32),
                        pltpu.VMEM((n_seq, d), F32)],
        compiler_params=_params("arbitrary"),
        name="mixer_sample",
    )(*args)


def _state_update_kernel(sa_ref, ss_ref, glu_ref, u_ref, nsa_ref, nss_ref):
    for st_ref, new_ref, o_ref in ((sa_ref, glu_ref, nsa_ref), (ss_ref, u_ref, nss_ref)):
        n_hist = st_ref.shape[0]
        for k in range(n_hist - 1):
            o_ref[k] = st_ref[k + 1]
        o_ref[n_hist - 1] = new_ref[...]


def _state_update(state_a_t, state_s_t, glu_all, u_all):
    depth, n_a, n_seq, da = state_a_t.shape
    n_b = state_s_t.shape[1]
    kc = SAMPLE_K
    plane = lambda n: pl.BlockSpec((None, n, n_seq, kc), lambda l, c: (l, 0, 0, c))
    new = pl.BlockSpec((None, n_seq, kc), lambda l, c: (l, 0, c))
    return pl.pallas_call(
        _state_update_kernel,
        grid=(depth, da // kc),
        in_specs=[plane(n_a), plane(n_b), new, new],
        out_specs=[plane(n_a), plane(n_b)],
        out_shape=[jax.ShapeDtypeStruct(state_a_t.shape, F32),
                   jax.ShapeDtypeStruct(state_s_t.shape, F32)],
        compiler_params=_params("arbitrary", "arbitrary"),
        name="state_update",
    )(state_a_t, state_s_t, glu_all, u_all)


def _silu(g):
    return g * jax.nn.sigmoid(g)


def _dense_ffn_kernel(xp_ref, xs_ref, fg_ref, wg_ref, wu_ref, wd_ref, op_ref, os_ref,
                      h_ref, hs_hi_ref, hs_lo_ref, *, n_pt):
    i, f = pl.program_id(0), pl.program_id(1)

    @pl.when((f == 0) & (i < n_pt))
    def _():
        x = xp_ref[...]
        h_ref[...] = _rms(x, fg_ref[...]).astype(BF16)
        op_ref[...] = x

    @pl.when(i < n_pt)
    def _():
        h = h_ref[...]
        g = _dot(h, wg_ref[...].astype(BF16))
        up = _dot(h, wu_ref[...].astype(BF16))
        op_ref[...] += _dot((_silu(g) * up).astype(BF16), wd_ref[...].astype(BF16))

    @pl.when((f == 0) & (i == n_pt))
    def _():
        x = xs_ref[...]
        hs_hi_ref[...], hs_lo_ref[...] = _split(_rms(x, fg_ref[...]))
        os_ref[...] = x

    @pl.when(i == n_pt)
    def _():
        h = (hs_hi_ref[...], hs_lo_ref[...])
        g = _dot3(h, _split(wg_ref[...]))
        up = _dot3(h, _split(wu_ref[...]))
        os_ref[...] += _dot3(_split(_silu(g) * up), _split(wd_ref[...]))


def _dense_ffn(x_p, x_s, fgain, wg, wu, wd, layer, j):
    n_p, d = x_p.shape
    n_s = x_s.shape[0]
    ff = wg.shape[-1]
    tm, tf = DENSE_ROWS, FF_COLS
    n_pt = n_p // tm
    prow = lambda i, f: (jnp.minimum(i, n_pt - 1), 0)
    return pl.pallas_call(
        functools.partial(_dense_ffn_kernel, n_pt=n_pt),
        grid=(n_pt + 1, ff // tf),
        in_specs=[pl.BlockSpec((tm, d), prow, pipeline_mode=pl.Buffered(1)),
                  pl.BlockSpec((n_s, d), lambda i, f: (0, 0)),
                  pl.BlockSpec((None, 1, d), lambda i, f: (layer, 0, 0)),
                  pl.BlockSpec((None, d, tf), lambda i, f: (j, 0, f)),
                  pl.BlockSpec((None, d, tf), lambda i, f: (j, 0, f)),
                  pl.BlockSpec((None, tf, d), lambda i, f: (j, f, 0))],
        out_specs=[pl.BlockSpec((tm, d), prow), pl.BlockSpec((n_s, d), lambda i, f: (0, 0))],
        out_shape=[jax.ShapeDtypeStruct((n_p, d), F32), jax.ShapeDtypeStruct((n_s, d), F32)],
        scratch_shapes=[pltpu.VMEM((tm, d), BF16), pltpu.VMEM((n_s, d), BF16),
                        pltpu.VMEM((n_s, d), BF16)],
        compiler_params=_params("arbitrary", "arbitrary"),
        name="dense_ffn",
    )(x_p, x_s, fgain, wg, wu, wd)


def _row_copy(src_hbm, dst_ref, src_row, dst_row, sem):
    return pltpu.make_async_copy(src_hbm.at[pl.ds(src_row, 1)], dst_ref.at[pl.ds(dst_row, 1)], sem)


def _expert_ffn_kernel(rid_ref, te_ref, sa_ref, h_hbm, sg_ref, wg_ref, wu_ref, wd_ref,
                       o_ref, land_ref, xs_ref, wgb_ref, wub_ref, wdb_ref, sem, *, n_sub, chunk):
    t, f = pl.program_id(0), pl.program_id(1)
    n_t, n_f = pl.num_programs(0), pl.num_programs(1)
    tm = xs_ref.shape[0]
    sub = tm // n_sub
    n_land = land_ref.shape[0]

    def fetch(tile, r):
        tok = rid_ref[tile * tm + jnp.minimum(r, tm - 1)]
        _row_copy(h_hbm, land_ref, tok, r, sem).start()

    def wait_landed():
        pltpu.make_async_copy(h_hbm.at[pl.ds(0, n_land)], land_ref, sem).wait()

    @pl.when(f == 0)
    def _():
        @pl.when(t == 0)
        def _():
            def body(r, carry):
                fetch(0, r)
                return carry

            lax.fori_loop(0, n_land, body, 0)

        wait_landed()
        xs_ref[...] = land_ref[0:tm, :].astype(BF16)
        o_ref[...] = jnp.zeros_like(o_ref)

    nxt = jnp.minimum(t + 1, n_t - 1)

    def prefetch():
        for r in range(chunk):
            fetch(nxt, f * chunk + r)

    def half(h):
        rows = slice(h * sub, (h + 1) * sub)
        x = xs_ref[rows, :]
        g = _dot(x, wgb_ref[...])
        up = _dot(x, wub_ref[...])
        o_ref[rows, :] += _dot((_silu(g) * up).astype(BF16), wdb_ref[...])

    active = [sa_ref[t * n_sub + h] == 1 for h in range(n_sub)]

    @pl.when(active[0])
    def _():
        wgb_ref[...] = wg_ref[...].astype(BF16)
        wub_ref[...] = wu_ref[...].astype(BF16)
        wdb_ref[...] = wd_ref[...].astype(BF16)
        prefetch()
        half(0)

    @pl.when(jnp.logical_not(active[0]))
    def _():
        prefetch()

    for h in range(1, n_sub):
        pl.when(active[h])(functools.partial(half, h))

    @pl.when(f == n_f - 1)
    def _():
        for h in range(n_sub):
            rows = slice(h * sub, (h + 1) * sub)

            @pl.when(active[h])
            def _():
                o_ref[rows, :] = sg_ref[rows, :] * o_ref[rows, :]

        @pl.when(t == n_t - 1)
        def _():
            wait_landed()


def _expert_ffn(row_ids, tile_expert, sub_active, h2, slot_gate, wg, wu, wd, j):
    s = row_ids.shape[0]
    d = h2.shape[1]
    ff = wg.shape[-1]
    sub, n_sub, tf = MOE_ROWS, MOE_SUBS, MOE_FF_COLS
    tm = sub * n_sub
    nf = ff // tf
    assert s % tm == 0
    chunk = -(-tm // nf)

    def fcol(f, t, sa):
        busy = sa[t * n_sub]
        return f * busy + (nf - 1) * (1 - busy)

    grid_spec = pltpu.PrefetchScalarGridSpec(
        num_scalar_prefetch=3,
        grid=(s // tm, nf),
        in_specs=[pl.BlockSpec(memory_space=pl.ANY),
                  pl.BlockSpec((tm, 1), lambda t, f, rid, te, sa: (t, 0)),
                  pl.BlockSpec((None, None, d, tf),
                               lambda t, f, rid, te, sa: (j, te[t], 0, fcol(f, t, sa))),
                  pl.BlockSpec((None, None, d, tf),
                               lambda t, f, rid, te, sa: (j, te[t], 0, fcol(f, t, sa))),
                  pl.BlockSpec((None, None, tf, d),
                               lambda t, f, rid, te, sa: (j, te[t], fcol(f, t, sa), 0))],
        out_specs=pl.BlockSpec((tm, d), lambda t, f, rid, te, sa: (t, 0)),
        scratch_shapes=[pltpu.VMEM((chunk * nf, d), F32), pltpu.VMEM((tm, d), BF16),
                        pltpu.VMEM((d, tf), BF16), pltpu.VMEM((d, tf), BF16),
                        pltpu.VMEM((tf, d), BF16), pltpu.SemaphoreType.DMA(())],
    )
    return pl.pallas_call(
        functools.partial(_expert_ffn_kernel, n_sub=n_sub, chunk=chunk),
        grid_spec=grid_spec,
        out_shape=jax.ShapeDtypeStruct((s, d), F32),
        compiler_params=_params("arbitrary", "arbitrary"),
        name="expert_ffn",
    )(row_ids, tile_expert, sub_active, h2, slot_gate, wg, wu, wd)


def _combine_kernel(pos_ref, xp_ref, xs_ref, g_ref, ys_hbm, op_ref, os_ref, buf_ref, sems, *,
                    n_pt, final):
    i = pl.program_id(0)
    n_i = pl.num_programs(0)
    tk = xp_ref.shape[0]
    slot = i % 2

    def fetch(tile, dst_slot):
        for r in range(tk):
            for k in range(TOP_K):
                _row_copy(ys_hbm, buf_ref.at[dst_slot, k], pos_ref[(tile * tk + r) * TOP_K + k], r,
                          sems.at[dst_slot]).start()

    def wait(src_slot):
        for k in range(TOP_K):
            pltpu.make_async_copy(ys_hbm.at[pl.ds(0, tk)], buf_ref.at[src_slot, k],
                                  sems.at[src_slot]).wait()

    @pl.when(i == 0)
    def _():
        fetch(0, 0)

    fetch(jnp.minimum(i + 1, n_i - 1), 1 - slot)
    wait(slot)

    def emit(x_ref, o_ref):
        x2 = x_ref[...] + (buf_ref[slot, 0] + buf_ref[slot, 1])
        o_ref[...] = _rms(x2, g_ref[...]) if final else x2

    @pl.when(i == n_i - 1)
    def _():
        wait(1 - slot)

    @pl.when(i < n_pt)
    def _():
        emit(xp_ref, op_ref)

    @pl.when(i == n_pt)
    def _():
        emit(xs_ref, os_ref)


def _combine(pos, x_p, x_s, ys, final_gain, final):
    n_p, d = x_p.shape
    n_s = x_s.shape[0]
    tk = COMBINE_ROWS
    assert n_s == tk and n_p % tk == 0
    n_pt = n_p // tk
    prow = lambda i, pos: (jnp.minimum(i, n_pt - 1), 0)
    grid_spec = pltpu.PrefetchScalarGridSpec(
        num_scalar_prefetch=1,
        grid=(n_pt + 1,),
        in_specs=[pl.BlockSpec((tk, d), prow),
                  pl.BlockSpec((n_s, d), lambda i, pos: (0, 0)),
                  pl.BlockSpec((1, d), lambda i, pos: (0, 0)),
                  pl.BlockSpec(memory_space=pl.ANY)],
        out_specs=[pl.BlockSpec((tk, d), prow), pl.BlockSpec((n_s, d), lambda i, pos: (0, 0))],
        scratch_shapes=[pltpu.VMEM((2, TOP_K, tk, d), F32), pltpu.SemaphoreType.DMA((2,))],
    )
    return pl.pallas_call(
        functools.partial(_combine_kernel, n_pt=n_pt, final=final),
        grid_spec=grid_spec,
        out_shape=[jax.ShapeDtypeStruct((n_p, d), F32), jax.ShapeDtypeStruct((n_s, d), F32)],
        compiler_params=_params("arbitrary"),
        name="combine",
    )(pos, x_p, x_s, final_gain, ys)


def _final_norm_kernel(x_ref, g_ref, o_ref):
    o_ref[...] = _rms(x_ref[...], g_ref[...])


def _final_norm(x, gain):
    n, d = x.shape
    tm = min(n, IN_ROWS)
    return pl.pallas_call(
        _final_norm_kernel,
        grid=(n // tm,),
        in_specs=[pl.BlockSpec((tm, d), lambda i: (i, 0)), pl.BlockSpec((1, d), lambda i: (0, 0))],
        out_specs=pl.BlockSpec((tm, d), lambda i: (i, 0)),
        out_shape=jax.ShapeDtypeStruct((n, d), F32),
        compiler_params=_params("arbitrary"),
        name="final_norm",
    )(x, gain)


def _slot_layout(top_idx, gates, n_experts, tm, n_sub):
    n = top_idx.shape[0]
    m = n * TOP_K
    n_tiles = (m + n_experts * (tm - 1)) // tm
    s = n_tiles * tm
    flat_e = top_idx.reshape(m)
    flat_g = gates.reshape(m)
    onehot = (flat_e[:, None] == jnp.arange(n_experts, dtype=jnp.int32)[None, :]).astype(jnp.int32)
    counts = jnp.sum(onehot, axis=0)
    padded = ((counts + tm - 1) // tm) * tm
    pad_end = jnp.cumsum(padded)
    pad_start = pad_end - padded
    cnt_start = jnp.cumsum(counts) - counts
    pos = jnp.sum(onehot * (jnp.cumsum(onehot, axis=0) + pad_start[None, :]), axis=1) - 1
    tile_start = jnp.arange(n_tiles, dtype=jnp.int32) * tm
    tile_valid = tile_start < pad_end[-1]
    tile_expert = jnp.minimum(jnp.sum((pad_end[None, :] <= tile_start[:, None]).astype(jnp.int32), axis=1),
                              n_experts - 1)
    order = jnp.argsort(flat_e, stable=True).astype(jnp.int32)
    local = (tile_start - pad_start[tile_expert])[:, None] + jnp.arange(tm, dtype=jnp.int32)[None, :]
    live = (local < counts[tile_expert][:, None]) & tile_valid[:, None]
    src = order[jnp.clip(cnt_start[tile_expert][:, None] + local, 0, m - 1).reshape(s)]
    live = live.reshape(s)
    row_ids = jnp.where(live, src // TOP_K, 0).astype(jnp.int32)
    slot_gate = jnp.where(live, flat_g[src], 0.0).astype(F32)
    sub_active = live.reshape(n_tiles * n_sub, tm // n_sub)[:, 0].astype(jnp.int32)
    return row_ids, slot_gate[:, None], pos.astype(jnp.int32), tile_expert, sub_active


def kernel(x_prompt, x_sample, state_conv_a, state_short, mix_norm, w_in, conv_a_w, conv_a_b, ln_a_g, ln_a_b, conv_b_w, w_out, ffn_norm, dense_w_gate, dense_w_up, dense_w_down, router, exp_w_gate, exp_w_up, exp_w_down, final_norm):
    batch, seq, d = x_prompt.shape
    n_seq, dec_seq, _ = x_sample.shape
    depth = w_in.shape[0]
    da = conv_a_w.shape[-1]
    ka, kb = conv_a_w.shape[1], conv_b_w.shape[1]
    n_experts = router.shape[-1]
    assert dec_seq == 1 and conv_b_w.shape[-1] == da and w_in.shape[-1] == 5 * da
    assert seq >= ka - 1 and seq >= kb - 1
    n_p = batch * seq

    x_p = x_prompt.reshape(n_p, d)
    x_s = x_sample.reshape(n_seq, d)
    mix_gain = mix_norm.reshape(depth, 1, d)
    ffn_gain = ffn_norm.reshape(depth, 1, d)
    cab = conv_a_b.reshape(depth, 1, da)
    lg = ln_a_g.reshape(depth, 1, da)
    lb = ln_a_b.reshape(depth, 1, da)
    final_gain = final_norm.reshape(1, d)
    w_out_bf = _to_bf16(w_out)
    state_a_t = jnp.transpose(state_conv_a, (0, 2, 1, 3))
    state_s_t = jnp.transpose(state_short, (0, 2, 1, 3))

    new_a_p, new_s_p, glu_s_all, u_s_all = [], [], [], []
    normed = False
    for l in range(depth):
        glu_p, u_p, bg_p, glu_s, u_s, bg_s = _in_proj(x_p, x_s, mix_gain, w_in, l, da)
        new_a_p.append(glu_p.reshape(batch, seq, da)[:, seq - (ka - 1):])
        new_s_p.append(u_p.reshape(batch, seq, da)[:, seq - (kb - 1):])
        j = l // 2
        routed = l % 2 == 1
        rt = router[j] if routed else None
        outs_p = _mixer_prompt(glu_p, u_p, bg_p, x_p, conv_a_w, cab, lg, lb, conv_b_w, w_out_bf,
                               ffn_gain, rt, l, batch, seq)
        outs_s = _mixer_sample(glu_s, u_s, bg_s, x_s, state_a_t, state_s_t, conv_a_w, cab, lg,
                               lb, conv_b_w, w_out, ffn_gain, rt, outs_p[1] if routed else None,
                               n_p, l)
        glu_s_all.append(glu_s)
        u_s_all.append(u_s)
        x1_p, x1_s = outs_p[0], outs_s[0]
        if not routed:
            x_p, x_s = _dense_ffn(x1_p, x1_s, ffn_gain, dense_w_gate, dense_w_up, dense_w_down, l, j)
        else:
            top_idx = jnp.concatenate([outs_p[2], outs_s[2]], axis=0)
            gates = jnp.concatenate([outs_p[3], outs_s[3]], axis=0)
            h2 = outs_s[1]
            row_ids, slot_gate, pos, tile_expert, sub_active = _slot_layout(
                top_idx, gates, n_experts, MOE_ROWS * MOE_SUBS, MOE_SUBS)
            ys = _expert_ffn(row_ids, tile_expert, sub_active, h2, slot_gate,
                             exp_w_gate, exp_w_up, exp_w_down, j)
            normed = l == depth - 1
            x_p, x_s = _combine(pos, x1_p, x1_s, ys, final_gain, normed)
    if not normed:
        x_p, x_s = _final_norm(x_p, final_gain), _final_norm(x_s, final_gain)
    new_a_t, new_s_t = _state_update(state_a_t, state_s_t, jnp.stack(glu_s_all), jnp.stack(u_s_all))
    return (x_p.reshape(batch, seq, d), x_s.reshape(n_seq, 1, d), jnp.stack(new_a_p),
            jnp.stack(new_s_p), jnp.transpose(new_a_t, (0, 2, 1, 3)),
            jnp.transpose(new_s_t, (0, 2, 1, 3)))
```

```python
import functools

import jax
import jax.numpy as jnp
from jax import lax
from jax.experimental import pallas as pl
from jax.experimental.pallas import tpu as pltpu

F32 = jnp.float32
BF16 = jnp.bfloat16

RMS_EPS = 1e-6
LN_EPS = 1e-5
TOP_K = 2

SUBLANES = 8
LANES = 128
VMEM_LIMIT_BYTES = 56 * 1024 * 1024

IN_ROWS = 512
IN_COLS = 256
MIX_ROWS = 256
CONV_STRIP_ROWS = 32
CONV_CHUNK = 64
SAMPLE_K = 256
DENSE_ROWS = 1024
FF_COLS = 256
MOE_BLOCKS = (384, 384, 192, 192)
MOE_FF_COLS = 256
COMBINE_ROWS = 128


def _params(*sem):
    return pltpu.CompilerParams(dimension_semantics=sem, vmem_limit_bytes=VMEM_LIMIT_BYTES)


def _rms(x, gain):
    y = x * lax.rsqrt(jnp.mean(x * x, axis=-1, keepdims=True) + RMS_EPS)
    return y * gain


def _dot(a, b):
    return jnp.dot(a, b, preferred_element_type=F32)


def _split(v):
    hi = v.astype(BF16)
    return hi, (v - hi.astype(F32)).astype(BF16)


def _dot3(a, b):
    a_hi, a_lo = a
    b_hi, b_lo = b
    return _dot(a_hi, b_hi) + (_dot(a_lo, b_hi) + _dot(a_hi, b_lo))


def _in_proj_kernel(xp_ref, xs_ref, g_ref, w0, w1, w2, w3, w4,
                    glu_p, u_p, bg_p, glu_s, u_s, bg_s, whi_ref, wlo_ref, *, tc, n_pt):
    i = pl.program_id(1)

    @pl.when(i == 0)
    def _():
        for k, w in enumerate((w0, w1, w2, w3, w4)):
            hi, lo = _split(w[...])
            whi_ref[:, k * tc:(k + 1) * tc] = hi
            wlo_ref[:, k * tc:(k + 1) * tc] = lo

    def emit(proj, glu_ref, u_ref, bg_ref):
        glu_ref[...] = proj[:, 0:tc] * jax.nn.sigmoid(proj[:, tc:2 * tc])
        bg_ref[...] = proj[:, 2 * tc:3 * tc]
        u_ref[...] = proj[:, 3 * tc:4 * tc] * proj[:, 4 * tc:5 * tc]

    @pl.when(i < n_pt)
    def _():
        h = _rms(xp_ref[...], g_ref[...]).astype(BF16)
        emit(_dot(h, whi_ref[...]), glu_p, u_p, bg_p)

    @pl.when(i == n_pt)
    def _():
        h = _split(_rms(xs_ref[...], g_ref[...]))
        emit(_dot3(h, (whi_ref[...], wlo_ref[...])), glu_s, u_s, bg_s)


def _in_proj(x_p, x_s, gain, w_in, layer, d_half):
    n_p, d = x_p.shape
    n_s = x_s.shape[0]
    tm, tc = IN_ROWS, IN_COLS
    nct, n_pt = d_half // tc, n_p // tm
    w_specs = [
        pl.BlockSpec((None, d, tc), functools.partial(lambda c, i, k: (layer, 0, k * nct + c), k=k))
        for k in range(5)
    ]
    prow = lambda c, i: jnp.minimum(i, n_pt - 1)
    out_p = jax.ShapeDtypeStruct((n_p, d_half), F32)
    out_s = jax.ShapeDtypeStruct((n_s, d_half), F32)
    op_spec = pl.BlockSpec((tm, tc), lambda c, i: (prow(c, i), c))
    os_spec = pl.BlockSpec((n_s, tc), lambda c, i: (0, c))
    return pl.pallas_call(
        functools.partial(_in_proj_kernel, tc=tc, n_pt=n_pt),
        grid=(nct, n_pt + 1),
        in_specs=[pl.BlockSpec((tm, d), lambda c, i: (prow(c, i), 0)),
                  pl.BlockSpec((n_s, d), lambda c, i: (0, 0)),
                  pl.BlockSpec((None, 1, d), lambda c, i: (layer, 0, 0))] + w_specs,
        out_specs=[op_spec, op_spec, op_spec, os_spec, os_spec, os_spec],
        out_shape=[out_p, out_p, out_p, out_s, out_s, out_s],
        scratch_shapes=[pltpu.VMEM((d, 5 * tc), BF16), pltpu.VMEM((d, 5 * tc), BF16)],
        compiler_params=_params("arbitrary", "arbitrary"),
        name="in_proj",
    )(x_p, x_s, gain, w_in, w_in, w_in, w_in, w_in)


def _cast_kernel(w_ref, o_ref):
    o_ref[...] = w_ref[...].astype(o_ref.dtype)


def _to_bf16(w):
    l, r, c = w.shape
    tr = min(r, 512)
    spec = pl.BlockSpec((None, tr, c), lambda i, j: (i, j, 0))
    return pl.pallas_call(
        _cast_kernel, grid=(l, r // tr), in_specs=[spec], out_specs=spec,
        out_shape=jax.ShapeDtypeStruct(w.shape, BF16),
        compiler_params=_params("arbitrary", "arbitrary"), name="cast_bf16",
    )(w)


def _route(logits, idx_ref, gate_ref):
    n_e = logits.shape[-1]
    lane = lax.broadcasted_iota(jnp.int32, logits.shape, 1)
    m1 = jnp.max(logits, axis=-1, keepdims=True)
    i1 = jnp.min(jnp.where(logits == m1, lane, n_e), axis=-1, keepdims=True)
    rest = jnp.where(lane == i1, -jnp.inf, logits)
    m2 = jnp.max(rest, axis=-1, keepdims=True)
    i2 = jnp.min(jnp.where(rest == m2, lane, n_e), axis=-1, keepdims=True)
    e2 = jnp.exp(m2 - m1)
    den = 1.0 + e2
    col = lax.broadcasted_iota(jnp.int32, (logits.shape[0], TOP_K), 1)
    idx_ref[...] = jnp.where(col == 0, i1, i2)
    gate_ref[...] = jnp.where(col == 0, 1.0 / den, e2 / den)


def _ln_silu(conv, cb_ref, lg_ref, lb_ref):
    v = conv + cb_ref[...]
    mu = jnp.mean(v, axis=-1, keepdims=True)
    var = jnp.mean(jnp.square(v - mu), axis=-1, keepdims=True)
    z = (v - mu) * lax.rsqrt(var + LN_EPS) * lg_ref[...] + lb_ref[...]
    return z * jax.nn.sigmoid(z)


def _shift_phases(offset, taps):
    return sorted({(offset + k) % SUBLANES for k in range(taps)} - {0})


def _mixer_prompt_kernel(*refs, tm, ka, kb, halo_a, halo_b, routed, n_batch):
    (glu_ref, gluh_ref, u_ref, uh_ref, bg_ref, x_ref, caw_ref, cab_ref, lg_ref, lb_ref, cbw_ref,
     wout_ref) = refs[:12]
    refs = refs[12:]
    if routed:
        fg_ref, router_ref, x1_ref, h2_ref, idx_ref, gate_ref = refs[:6]
        refs = refs[6:]
    else:
        x1_ref = refs[0]
        refs = refs[1:]
    win_ref, sha_ref, uwin_ref, shb_ref, ca_ref, y_ref = refs
    b, t = pl.program_id(0), pl.program_id(1)
    da = glu_ref.shape[-1]

    def body():
        first = t == 0
        win_ref[0:halo_a, :] = jnp.where(first, 0.0, gluh_ref[...])
        win_ref[halo_a:halo_a + tm, :] = glu_ref[...]
        uwin_ref[0:halo_b, :] = jnp.where(first, 0.0, uh_ref[...])
        uwin_ref[halo_b:halo_b + tm, :] = u_ref[...]

        off_a = halo_a - (ka - 1)
        off_b = halo_b - (kb - 1)
        phases_a = _shift_phases(off_a, ka)
        phases_b = _shift_phases(off_b, kb)
        len_a = halo_a + tm - SUBLANES
        for j, p in enumerate(phases_a):
            sha_ref[j, 0:len_a, :] = win_ref[p:p + len_a, :]
        len_b = halo_b + tm - SUBLANES
        for j, p in enumerate(phases_b):
            shb_ref[j, 0:len_b, :] = uwin_ref[p:p + len_b, :]

        def tap(base_ref, shift_ref, phases, offset, r0, n_rows, cols):
            q, p = divmod(offset, SUBLANES)
            rows = pl.ds(r0 + q * SUBLANES, n_rows)
            return base_ref[rows, cols] if p == 0 else shift_ref[phases.index(p), rows, cols]

        strip_rows = CONV_STRIP_ROWS
        for lane0 in range(0, da, LANES):
            cols = slice(lane0, lane0 + LANES)
            taps = [jnp.broadcast_to(caw_ref[k:k + 1, cols], (SUBLANES, LANES)) for k in range(ka)]

            def strip(c, carry):
                r0 = pl.multiple_of(c * strip_rows, strip_rows)
                for g in range(0, strip_rows, SUBLANES):
                    acc = taps[0] * tap(win_ref, sha_ref, phases_a, off_a, r0 + g, SUBLANES, cols)
                    for k in range(1, ka):
                        acc = acc + taps[k] * tap(win_ref, sha_ref, phases_a, off_a + k, r0 + g,
                                                  SUBLANES, cols)
                    ca_ref[pl.ds(r0 + g, SUBLANES), cols] = acc
                return carry

            lax.fori_loop(0, tm // strip_rows, strip, 0)

        rb = CONV_CHUNK
        full = slice(None)

        def chunk(c, carry):
            r0 = pl.multiple_of(c * rb, rb)
            rows = pl.ds(r0, rb)
            y_ref[rows, 0:da] = _ln_silu(ca_ref[rows, :], cab_ref, lg_ref, lb_ref).astype(BF16)
            accb = jnp.zeros((rb, da), F32)
            for k in range(kb):
                accb = accb + cbw_ref[k:k + 1, :] * tap(uwin_ref, shb_ref, phases_b, off_b + k, r0,
                                                        rb, full)
            y_ref[rows, da:2 * da] = (bg_ref[rows, :] * accb).astype(BF16)
            return carry

        lax.fori_loop(0, tm // rb, chunk, 0)

        x1 = x_ref[...] + _dot(y_ref[...], wout_ref[...])
        x1_ref[...] = x1
        if routed:
            h2 = _rms(x1, fg_ref[...])
            h2_ref[...] = h2
            _route(_dot(h2.astype(BF16), router_ref[...].astype(BF16)), idx_ref, gate_ref)

    if not routed:
        body()
    else:
        pl.when(b < n_batch)(body)

        @pl.when((b == n_batch) & (t == 0))
        def _():
            h2_ref[...] = jnp.zeros_like(h2_ref)


def _mixer_prompt(glu, u, bg, x, caw, cab, lg, lb, cbw, w_out_bf, fgain, router, layer, batch, seq):
    d = x.shape[1]
    da = glu.shape[1]
    ka, kb = caw.shape[1], cbw.shape[1]
    tm = MIX_ROWS
    nt = seq // tm
    halo_a, halo_b = 32, SUBLANES
    assert ka - 1 <= halo_a and kb - 1 <= halo_b and seq % tm == 0 and tm % halo_a == 0
    routed = router is not None
    n_rows = batch * seq
    n_blocks = batch * nt

    def blk(b, t):
        return jnp.minimum(b * nt + t, n_blocks - 1)

    def row(b, t):
        return (blk(b, t), 0)

    def halo(width):
        per = tm // width
        return lambda b, t: (jnp.maximum(blk(b, t) * per - 1, 0), 0)

    vec = lambda n: pl.BlockSpec((None, 1, n), lambda b, t: (layer, 0, 0))
    in_specs = [
        pl.BlockSpec((tm, da), row), pl.BlockSpec((halo_a, da), halo(halo_a)),
        pl.BlockSpec((tm, da), row), pl.BlockSpec((halo_b, da), halo(halo_b)),
        pl.BlockSpec((tm, da), row), pl.BlockSpec((tm, d), row),
        pl.BlockSpec((None, ka, da), lambda b, t: (layer, 0, 0)), vec(da), vec(da), vec(da),
        pl.BlockSpec((None, kb, da), lambda b, t: (layer, 0, 0)),
        pl.BlockSpec((None, d, d), lambda b, t: (layer, 0, 0)),
    ]
    args = [glu, glu, u, u, bg, x, caw, cab, lg, lb, cbw, w_out_bf]
    out_shape = [jax.ShapeDtypeStruct((n_rows, d), F32)]
    out_specs = [pl.BlockSpec((tm, d), row)]
    if routed:
        n_e = router.shape[-1]
        in_specs += [vec(d), pl.BlockSpec((d, n_e), lambda b, t: (0, 0))]
        args += [fgain, router]
        out_shape += [jax.ShapeDtypeStruct((n_rows + tm, d), F32),
                      jax.ShapeDtypeStruct((n_rows, TOP_K), jnp.int32),
                      jax.ShapeDtypeStruct((n_rows, TOP_K), F32)]
        out_specs += [pl.BlockSpec((tm, d), lambda b, t: (jnp.minimum(b * nt + t, n_blocks), 0)),
                      pl.BlockSpec((tm, TOP_K), row), pl.BlockSpec((tm, TOP_K), row)]
    n_phase_a = len(_shift_phases(halo_a - (ka - 1), ka))
    n_phase_b = len(_shift_phases(halo_b - (kb - 1), kb))
    return pl.pallas_call(
        functools.partial(_mixer_prompt_kernel, tm=tm, ka=ka, kb=kb, halo_a=halo_a, halo_b=halo_b,
                          routed=routed, n_batch=batch),
        grid=(batch + 1 if routed else batch, nt),
        in_specs=in_specs, out_specs=out_specs, out_shape=out_shape,
        scratch_shapes=[pltpu.VMEM((halo_a + tm, da), F32),
                        pltpu.VMEM((n_phase_a, halo_a + tm, da), F32),
                        pltpu.VMEM((halo_b + tm, da), F32),
                        pltpu.VMEM((n_phase_b, halo_b + tm, da), F32),
                        pltpu.VMEM((tm, da), F32),
                        pltpu.VMEM((tm, d), BF16)],
        compiler_params=_params("arbitrary", "arbitrary"),
        name="mixer_prompt",
    )(*args)


def _mixer_sample_kernel(*refs, ka, kb, n_blk, n_k, routed):
    (glu_ref, u_ref, bg_ref, x_ref, sa_ref, ss_ref, caw_ref, cab_ref, lg_ref, lb_ref, cbw_ref,
     wout_ref) = refs[:12]
    refs = refs[12:]
    if routed:
        fg_ref, router_ref, _ = refs[:3]
        refs = refs[3:]
    x1_ref = refs[0]
    refs = refs[1:]
    if routed:
        h2_ref, idx_ref, gate_ref = refs[:3]
        refs = refs[3:]
    ca_ref, y_ref, acc_ref = refs
    s = pl.program_id(0)
    n_seq, kc = glu_ref.shape
    da = n_blk * kc

    @pl.when(s < n_blk)
    def _():
        conv = caw_ref[ka - 1:ka, :] * glu_ref[...]
        for k in range(ka - 1):
            conv = conv + caw_ref[k:k + 1, :] * sa_ref[k]
        ca_ref[s] = conv
        conv_b = cbw_ref[kb - 1:kb, :] * u_ref[...]
        for k in range(kb - 1):
            conv_b = conv_b + cbw_ref[k:k + 1, :] * ss_ref[k]
        y_ref[n_blk + s] = bg_ref[...] * conv_b

    @pl.when(s == n_blk - 1)
    def _():
        tiles = [ca_ref[c] + cab_ref[:, c * kc:(c + 1) * kc] for c in range(n_blk)]
        mu = sum(jnp.sum(v, axis=-1, keepdims=True) for v in tiles) / da
        var = sum(jnp.sum(jnp.square(v - mu), axis=-1, keepdims=True) for v in tiles) / da
        inv = lax.rsqrt(var + LN_EPS)
        for c in range(n_blk):
            cols = slice(c * kc, (c + 1) * kc)
            z = (tiles[c] - mu) * inv * lg_ref[:, cols] + lb_ref[:, cols]
            y_ref[c] = z * jax.nn.sigmoid(z)

    @pl.when(s == n_blk)
    def _():
        acc_ref[...] = x_ref[...]

    @pl.when(s >= n_blk)
    def _():
        acc_ref[...] += _dot3(_split(y_ref[s - n_blk]), _split(wout_ref[...]))

    @pl.when(s == n_blk + n_k - 1)
    def _():
        x1 = acc_ref[...]
        x1_ref[...] = x1
        if routed:
            h2 = _rms(x1, fg_ref[...])
            h2_ref[...] = h2
            _route(_dot3(_split(h2), _split(router_ref[...])), idx_ref, gate_ref)


def _mixer_sample(glu, u, bg, x, state_a_t, state_s_t, caw, cab, lg, lb, cbw, w_out, fgain, router,
                  h2_all, h2_row0, layer):
    n_seq, d = x.shape
    da = glu.shape[1]
    ka, kb = caw.shape[1], cbw.shape[1]
    kc = SAMPLE_K
    n_blk, n_k = da // kc, d // kc
    assert da % kc == 0 and d == 2 * da
    routed = router is not None
    full = lambda n: pl.BlockSpec((n_seq, n), lambda s: (0, 0))
    vec = lambda n: pl.BlockSpec((None, 1, n), lambda s: (layer, 0, 0))
    blk = lambda s: jnp.minimum(s, n_blk - 1)
    tile = pl.BlockSpec((n_seq, kc), lambda s: (0, blk(s)))
    in_specs = [
        tile, tile, tile, full(d),
        pl.BlockSpec((None, ka - 1, n_seq, kc), lambda s: (layer, 0, 0, blk(s))),
        pl.BlockSpec((None, kb - 1, n_seq, kc), lambda s: (layer, 0, 0, blk(s))),
        pl.BlockSpec((None, ka, kc), lambda s: (layer, 0, blk(s))), vec(da), vec(da), vec(da),
        pl.BlockSpec((None, kb, kc), lambda s: (layer, 0, blk(s))),
        pl.BlockSpec((None, kc, d), lambda s: (layer, jnp.maximum(s - n_blk, 0), 0)),
    ]
    args = [glu, u, bg, x, state_a_t, state_s_t, caw, cab, lg, lb, cbw, w_out]
    out_shape = [jax.ShapeDtypeStruct((n_seq, d), F32)]
    out_specs = [full(d)]
    aliases = {}
    if routed:
        n_e = router.shape[-1]
        assert h2_row0 % n_seq == 0 and h2_row0 + n_seq <= h2_all.shape[0]
        in_specs += [vec(d), pl.BlockSpec((d, n_e), lambda s: (0, 0)),
                     pl.BlockSpec(memory_space=pl.ANY)]
        args += [fgain, router, h2_all]
        aliases = {len(args) - 1: 1}
        out_shape += [jax.ShapeDtypeStruct(h2_all.shape, F32),
                      jax.ShapeDtypeStruct((n_seq, TOP_K), jnp.int32),
                      jax.ShapeDtypeStruct((n_seq, TOP_K), F32)]
        out_specs += [pl.BlockSpec((n_seq, d), lambda s: (h2_row0 // n_seq, 0)),
                      full(TOP_K), full(TOP_K)]
    return pl.pallas_call(
        functools.partial(_mixer_sample_kernel, ka=ka, kb=kb, n_blk=n_blk, n_k=n_k, routed=routed),
        grid=(n_blk + n_k,),
        in_specs=in_specs, out_specs=out_specs, out_shape=out_shape,
        input_output_aliases=aliases,
        scratch_shapes=[pltpu.VMEM((n_blk, n_seq, kc), F32),
                        pltpu.VMEM((n_k, n_seq, kc), F32),
                        pltpu.VMEM((n_seq, d), F32)],
        compiler_params=_params("arbitrary"),
        name="mixer_sample",
    )(*args)


def _state_update_kernel(sa_ref, ss_ref, glu_ref, u_ref, nsa_ref, nss_ref):
    for st_ref, new_ref, o_ref in ((sa_ref, glu_ref, nsa_ref), (ss_ref, u_ref, nss_ref)):
        n_hist = st_ref.shape[0]
        for k in range(n_hist - 1):
            o_ref[k] = st_ref[k + 1]
        o_ref[n_hist - 1] = new_ref[...]


def _state_update(state_a_t, state_s_t, glu_all, u_all):
    depth, n_a, n_seq, da = state_a_t.shape
    n_b = state_s_t.shape[1]
    kc = SAMPLE_K
    plane = lambda n: pl.BlockSpec((None, n, n_seq, kc), lambda l, c: (l, 0, 0, c))
    new = pl.BlockSpec((None, n_seq, kc), lambda l, c: (l, 0, c))
    return pl.pallas_call(
        _state_update_kernel,
        grid=(depth, da // kc),
        in_specs=[plane(n_a), plane(n_b), new, new],
        out_specs=[plane(n_a), plane(n_b)],
        out_shape=[jax.ShapeDtypeStruct(state_a_t.shape, F32),
                   jax.ShapeDtypeStruct(state_s_t.shape, F32)],
        compiler_params=_params("arbitrary", "arbitrary"),
        name="state_update",
    )(state_a_t, state_s_t, glu_all, u_all)


def _silu(g):
    return g * jax.nn.sigmoid(g)


def _dense_ffn_kernel(xp_ref, xs_ref, fg_ref, wg_ref, wu_ref, wd_ref, op_ref, os_ref,
                      h_ref, hs_hi_ref, hs_lo_ref, *, n_pt):
    i, f = pl.program_id(0), pl.program_id(1)

    @pl.when((f == 0) & (i < n_pt))
    def _():
        x = xp_ref[...]
        h_ref[...] = _rms(x, fg_ref[...]).astype(BF16)
        op_ref[...] = x

    @pl.when(i < n_pt)
    def _():
        h = h_ref[...]
        g = _dot(h, wg_ref[...].astype(BF16))
        up = _dot(h, wu_ref[...].astype(BF16))
        op_ref[...] += _dot((_silu(g) * up).astype(BF16), wd_ref[...].astype(BF16))

    @pl.when((f == 0) & (i == n_pt))
    def _():
        x = xs_ref[...]
        hs_hi_ref[...], hs_lo_ref[...] = _split(_rms(x, fg_ref[...]))
        os_ref[...] = x

    @pl.when(i == n_pt)
    def _():
        h = (hs_hi_ref[...], hs_lo_ref[...])
        g = _dot3(h, _split(wg_ref[...]))
        up = _dot3(h, _split(wu_ref[...]))
        os_ref[...] += _dot3(_split(_silu(g) * up), _split(wd_ref[...]))


def _dense_ffn(x_p, x_s, fgain, wg, wu, wd, layer, j):
    n_p, d = x_p.shape
    n_s = x_s.shape[0]
    ff = wg.shape[-1]
    tm, tf = DENSE_ROWS, FF_COLS
    n_pt = n_p // tm
    prow = lambda i, f: (jnp.minimum(i, n_pt - 1), 0)
    return pl.pallas_call(
        functools.partial(_dense_ffn_kernel, n_pt=n_pt),
        grid=(n_pt + 1, ff // tf),
        in_specs=[pl.BlockSpec((tm, d), prow, pipeline_mode=pl.Buffered(1)),
                  pl.BlockSpec((n_s, d), lambda i, f: (0, 0)),
                  pl.BlockSpec((None, 1, d), lambda i, f: (layer, 0, 0)),
                  pl.BlockSpec((None, d, tf), lambda i, f: (j, 0, f)),
                  pl.BlockSpec((None, d, tf), lambda i, f: (j, 0, f)),
                  pl.BlockSpec((None, tf, d), lambda i, f: (j, f, 0))],
        out_specs=[pl.BlockSpec((tm, d), prow), pl.BlockSpec((n_s, d), lambda i, f: (0, 0))],
        out_shape=[jax.ShapeDtypeStruct((n_p, d), F32), jax.ShapeDtypeStruct((n_s, d), F32)],
        scratch_shapes=[pltpu.VMEM((tm, d), BF16), pltpu.VMEM((n_s, d), BF16),
                        pltpu.VMEM((n_s, d), BF16)],
        compiler_params=_params("arbitrary", "arbitrary"),
        name="dense_ffn",
    )(x_p, x_s, fgain, wg, wu, wd)


def _row_copy(src_hbm, dst_ref, src_row, dst_row, sem):
    return pltpu.make_async_copy(src_hbm.at[pl.ds(src_row, 1)], dst_ref.at[pl.ds(dst_row, 1)], sem)


def _expert_ffn_kernel(rid_ref, te_ref, sa_ref, h_hbm, sg_ref, wg_ref, wu_ref, wd_ref,
                       o_ref, land_ref, xs_ref, wgb_ref, wub_ref, wdb_ref, sem, *, blocks, chunk):
    t, f = pl.program_id(0), pl.program_id(1)
    n_t, n_f = pl.num_programs(0), pl.num_programs(1)
    tm = xs_ref.shape[0]
    n_land = land_ref.shape[0]

    def fetch(tile, r):
        tok = rid_ref[tile * tm + jnp.minimum(r, tm - 1)]
        _row_copy(h_hbm, land_ref, tok, r, sem).start()

    def wait_landed():
        pltpu.make_async_copy(h_hbm.at[pl.ds(0, n_land)], land_ref, sem).wait()

    @pl.when(f == 0)
    def _():
        @pl.when(t == 0)
        def _():
            def body(r, carry):
                fetch(0, r)
                return carry

            lax.fori_loop(0, n_land, body, 0)

        wait_landed()
        xs_ref[...] = land_ref[0:tm, :].astype(BF16)
        o_ref[...] = jnp.zeros_like(o_ref)

    nxt = jnp.minimum(t + 1, n_t - 1)

    def prefetch():
        for r in range(chunk):
            fetch(nxt, f * chunk + r)

    def swiglu(rows):
        x = xs_ref[rows, :]
        g = _dot(x, wgb_ref[...])
        up = _dot(x, wub_ref[...])
        o_ref[rows, :] += _dot((_silu(g) * up).astype(BF16), wdb_ref[...])

    n_blk = len(blocks)
    starts = [sum(blocks[:b]) for b in range(n_blk)]
    active = [sa_ref[t * n_blk + b] == 1 for b in range(n_blk)]
    partial = jnp.logical_not(active[-1])

    @pl.when(active[0])
    def _():
        wgb_ref[...] = wg_ref[...].astype(BF16)
        wub_ref[...] = wu_ref[...].astype(BF16)
        wdb_ref[...] = wd_ref[...].astype(BF16)
        prefetch()

    @pl.when(jnp.logical_not(active[0]))
    def _():
        prefetch()

    @pl.when(active[-1])
    def _():
        swiglu(slice(0, tm))

    for b in range(n_blk - 1):
        pl.when(active[b] & partial)(functools.partial(swiglu, slice(starts[b], starts[b] + blocks[b])))

    @pl.when(f == n_f - 1)
    def _():
        for b in range(n_blk):
            rows = slice(starts[b], starts[b] + blocks[b])

            @pl.when(active[b])
            def _():
                o_ref[rows, :] = sg_ref[rows, :] * o_ref[rows, :]

        @pl.when(t == n_t - 1)
        def _():
            wait_landed()


def _expert_ffn(row_ids, tile_expert, sub_active, h2, slot_gate, wg, wu, wd, j):
    s = row_ids.shape[0]
    d = h2.shape[1]
    ff = wg.shape[-1]
    blocks, tf = MOE_BLOCKS, MOE_FF_COLS
    tm = sum(blocks)
    nf = ff // tf
    assert s % tm == 0
    chunk = -(-tm // nf)

    def fcol(f, t, sa):
        busy = sa[t * len(blocks)]
        return f * busy + (nf - 1) * (1 - busy)

    grid_spec = pltpu.PrefetchScalarGridSpec(
        num_scalar_prefetch=3,
        grid=(s // tm, nf),
        in_specs=[pl.BlockSpec(memory_space=pl.ANY),
                  pl.BlockSpec((tm, 1), lambda t, f, rid, te, sa: (t, 0)),
                  pl.BlockSpec((None, None, d, tf),
                               lambda t, f, rid, te, sa: (j, te[t], 0, fcol(f, t, sa))),
                  pl.BlockSpec((None, None, d, tf),
                               lambda t, f, rid, te, sa: (j, te[t], 0, fcol(f, t, sa))),
                  pl.BlockSpec((None, None, tf, d),
                               lambda t, f, rid, te, sa: (j, te[t], fcol(f, t, sa), 0))],
        out_specs=pl.BlockSpec((tm, d), lambda t, f, rid, te, sa: (t, 0)),
        scratch_shapes=[pltpu.VMEM((chunk * nf, d), F32), pltpu.VMEM((tm, d), BF16),
                        pltpu.VMEM((d, tf), BF16), pltpu.VMEM((d, tf), BF16),
                        pltpu.VMEM((tf, d), BF16), pltpu.SemaphoreType.DMA(())],
    )
    return pl.pallas_call(
        functools.partial(_expert_ffn_kernel, blocks=blocks, chunk=chunk),
        grid_spec=grid_spec,
        out_shape=jax.ShapeDtypeStruct((s, d), F32),
        compiler_params=_params("arbitrary", "arbitrary"),
        name="expert_ffn",
    )(row_ids, tile_expert, sub_active, h2, slot_gate, wg, wu, wd)


def _combine_kernel(pos_ref, xp_ref, xs_ref, g_ref, ys_hbm, op_ref, os_ref, buf_ref, sems, *,
                    n_pt, final):
    i = pl.program_id(0)
    n_i = pl.num_programs(0)
    tk = xp_ref.shape[0]
    slot = i % 2

    def fetch(tile, dst_slot):
        for r in range(tk):
            for k in range(TOP_K):
                _row_copy(ys_hbm, buf_ref.at[dst_slot, k], pos_ref[(tile * tk + r) * TOP_K + k], r,
                          sems.at[dst_slot]).start()

    def wait(src_slot):
        for k in range(TOP_K):
            pltpu.make_async_copy(ys_hbm.at[pl.ds(0, tk)], buf_ref.at[src_slot, k],
                                  sems.at[src_slot]).wait()

    @pl.when(i == 0)
    def _():
        fetch(0, 0)

    fetch(jnp.minimum(i + 1, n_i - 1), 1 - slot)
    wait(slot)

    def emit(x_ref, o_ref):
        x2 = x_ref[...] + (buf_ref[slot, 0] + buf_ref[slot, 1])
        o_ref[...] = _rms(x2, g_ref[...]) if final else x2

    @pl.when(i == n_i - 1)
    def _():
        wait(1 - slot)

    @pl.when(i < n_pt)
    def _():
        emit(xp_ref, op_ref)

    @pl.when(i == n_pt)
    def _():
        emit(xs_ref, os_ref)


def _combine(pos, x_p, x_s, ys, final_gain, final):
    n_p, d = x_p.shape
    n_s = x_s.shape[0]
    tk = COMBINE_ROWS
    assert n_s == tk and n_p % tk == 0
    n_pt = n_p // tk
    prow = lambda i, pos: (jnp.minimum(i, n_pt - 1), 0)
    grid_spec = pltpu.PrefetchScalarGridSpec(
        num_scalar_prefetch=1,
        grid=(n_pt + 1,),
        in_specs=[pl.BlockSpec((tk, d), prow),
                  pl.BlockSpec((n_s, d), lambda i, pos: (0, 0)),
                  pl.BlockSpec((1, d), lambda i, pos: (0, 0)),
                  pl.BlockSpec(memory_space=pl.ANY)],
        out_specs=[pl.BlockSpec((tk, d), prow), pl.BlockSpec((n_s, d), lambda i, pos: (0, 0))],
        scratch_shapes=[pltpu.VMEM((2, TOP_K, tk, d), F32), pltpu.SemaphoreType.DMA((2,))],
    )
    return pl.pallas_call(
        functools.partial(_combine_kernel, n_pt=n_pt, final=final),
        grid_spec=grid_spec,
        out_shape=[jax.ShapeDtypeStruct((n_p, d), F32), jax.ShapeDtypeStruct((n_s, d), F32)],
        compiler_params=_params("arbitrary"),
        name="combine",
    )(pos, x_p, x_s, final_gain, ys)


def _final_norm_kernel(x_ref, g_ref, o_ref):
    o_ref[...] = _rms(x_ref[...], g_ref[...])


def _final_norm(x, gain):
    n, d = x.shape
    tm = min(n, IN_ROWS)
    return pl.pallas_call(
        _final_norm_kernel,
        grid=(n // tm,),
        in_specs=[pl.BlockSpec((tm, d), lambda i: (i, 0)), pl.BlockSpec((1, d), lambda i: (0, 0))],
        out_specs=pl.BlockSpec((tm, d), lambda i: (i, 0)),
        out_shape=jax.ShapeDtypeStruct((n, d), F32),
        compiler_params=_params("arbitrary"),
        name="final_norm",
    )(x, gain)


def _slot_layout(top_idx, gates, n_experts, blocks):
    tm = sum(blocks)
    n = top_idx.shape[0]
    m = n * TOP_K
    n_tiles = (m + n_experts * (tm - 1)) // tm
    s = n_tiles * tm
    flat_e = top_idx.reshape(m)
    flat_g = gates.reshape(m)
    onehot = (flat_e[:, None] == jnp.arange(n_experts, dtype=jnp.int32)[None, :]).astype(jnp.int32)
    counts = jnp.sum(onehot, axis=0)
    padded = ((counts + tm - 1) // tm) * tm
    pad_end = jnp.cumsum(padded)
    pad_start = pad_end - padded
    cnt_start = jnp.cumsum(counts) - counts
    pos = jnp.sum(onehot * (jnp.cumsum(onehot, axis=0) + pad_start[None, :]), axis=1) - 1
    tile_start = jnp.arange(n_tiles, dtype=jnp.int32) * tm
    tile_valid = tile_start < pad_end[-1]
    tile_expert = jnp.minimum(jnp.sum((pad_end[None, :] <= tile_start[:, None]).astype(jnp.int32), axis=1),
                              n_experts - 1)
    order = jnp.argsort(flat_e, stable=True).astype(jnp.int32)
    local = (tile_start - pad_start[tile_expert])[:, None] + jnp.arange(tm, dtype=jnp.int32)[None, :]
    live = (local < counts[tile_expert][:, None]) & tile_valid[:, None]
    src = order[jnp.clip(cnt_start[tile_expert][:, None] + local, 0, m - 1).reshape(s)]
    live = live.reshape(s)
    row_ids = jnp.where(live, src // TOP_K, 0).astype(jnp.int32)
    slot_gate = jnp.where(live, flat_g[src], 0.0).astype(F32)
    block_starts = [sum(blocks[:b]) for b in range(len(blocks))]
    sub_active = live.reshape(n_tiles, tm)[:, block_starts].reshape(-1).astype(jnp.int32)
    return row_ids, slot_gate[:, None], pos.astype(jnp.int32), tile_expert, sub_active


def kernel(x_prompt, x_sample, state_conv_a, state_short, mix_norm, w_in, conv_a_w, conv_a_b, ln_a_g, ln_a_b, conv_b_w, w_out, ffn_norm, dense_w_gate, dense_w_up, dense_w_down, router, exp_w_gate, exp_w_up, exp_w_down, final_norm):
    batch, seq, d = x_prompt.shape
    n_seq, dec_seq, _ = x_sample.shape
    depth = w_in.shape[0]
    da = conv_a_w.shape[-1]
    ka, kb = conv_a_w.shape[1], conv_b_w.shape[1]
    n_experts = router.shape[-1]
    assert dec_seq == 1 and conv_b_w.shape[-1] == da and w_in.shape[-1] == 5 * da
    assert seq >= ka - 1 and seq >= kb - 1
    n_p = batch * seq

    x_p = x_prompt.reshape(n_p, d)
    x_s = x_sample.reshape(n_seq, d)
    mix_gain = mix_norm.reshape(depth, 1, d)
    ffn_gain = ffn_norm.reshape(depth, 1, d)
    cab = conv_a_b.reshape(depth, 1, da)
    lg = ln_a_g.reshape(depth, 1, da)
    lb = ln_a_b.reshape(depth, 1, da)
    final_gain = final_norm.reshape(1, d)
    w_out_bf = _to_bf16(w_out)
    state_a_t = jnp.transpose(state_conv_a, (0, 2, 1, 3))
    state_s_t = jnp.transpose(state_short, (0, 2, 1, 3))

    new_a_p, new_s_p, glu_s_all, u_s_all = [], [], [], []
    normed = False
    for l in range(depth):
        glu_p, u_p, bg_p, glu_s, u_s, bg_s = _in_proj(x_p, x_s, mix_gain, w_in, l, da)
        new_a_p.append(glu_p.reshape(batch, seq, da)[:, seq - (ka - 1):])
        new_s_p.append(u_p.reshape(batch, seq, da)[:, seq - (kb - 1):])
        j = l // 2
        routed = l % 2 == 1
        rt = router[j] if routed else None
        outs_p = _mixer_prompt(glu_p, u_p, bg_p, x_p, conv_a_w, cab, lg, lb, conv_b_w, w_out_bf,
                               ffn_gain, rt, l, batch, seq)
        outs_s = _mixer_sample(glu_s, u_s, bg_s, x_s, state_a_t, state_s_t, conv_a_w, cab, lg,
                               lb, conv_b_w, w_out, ffn_gain, rt, outs_p[1] if routed else None,
                               n_p, l)
        glu_s_all.append(glu_s)
        u_s_all.append(u_s)
        x1_p, x1_s = outs_p[0], outs_s[0]
        if not routed:
            x_p, x_s = _dense_ffn(x1_p, x1_s, ffn_gain, dense_w_gate, dense_w_up, dense_w_down, l, j)
        else:
            top_idx = jnp.concatenate([outs_p[2], outs_s[2]], axis=0)
            gates = jnp.concatenate([outs_p[3], outs_s[3]], axis=0)
            h2 = outs_s[1]
            row_ids, slot_gate, pos, tile_expert, sub_active = _slot_layout(
                top_idx, gates, n_experts, MOE_BLOCKS)
            ys = _expert_ffn(row_ids, tile_expert, sub_active, h2, slot_gate,
                             exp_w_gate, exp_w_up, exp_w_down, j)
            normed = l == depth - 1
            x_p, x_s = _combine(pos, x1_p, x1_s, ys, final_gain, normed)
    if not normed:
        x_p, x_s = _final_norm(x_p, final_gain), _final_norm(x_s, final_gain)
    new_a_t, new_s_t = _state_update(state_a_t, state_s_t, jnp.stack(glu_s_all), jnp.stack(u_s_all))
    return (x_p.reshape(batch, seq, d), x_s.reshape(n_seq, 1, d), jnp.stack(new_a_p),
            jnp.stack(new_s_p), jnp.transpose(new_a_t, (0, 2, 1, 3)),
            jnp.transpose(new_s_t, (0, 2, 1, 3)))
```

```python
import functools

import jax
import jax.numpy as jnp
from jax import lax
from jax.experimental import pallas as pl
from jax.experimental.pallas import tpu as pltpu

F32 = jnp.float32
BF16 = jnp.bfloat16

RMS_EPS = 1e-6
LN_EPS = 1e-5
TOP_K = 2

SUBLANES = 8
LANES = 128
VMEM_LIMIT_BYTES = 56 * 1024 * 1024

IN_ROWS = 512
IN_COLS = 256
MIX_ROWS = 256
CONV_STRIP_ROWS = 64
CONV_CHUNK = 128
SAMPLE_K = 256
DENSE_ROWS = 1024
FF_COLS = 256
MOE_BLOCKS = (384, 384, 192, 192)
MOE_FF_COLS = 256
COMBINE_ROWS = 128


def _params(*sem):
    return pltpu.CompilerParams(dimension_semantics=sem, vmem_limit_bytes=VMEM_LIMIT_BYTES)


def _rms(x, gain):
    y = x * lax.rsqrt(jnp.mean(x * x, axis=-1, keepdims=True) + RMS_EPS)
    return y * gain


def _dot(a, b):
    return jnp.dot(a, b, preferred_element_type=F32)


def _split(v):
    hi = v.astype(BF16)
    return hi, (v - hi.astype(F32)).astype(BF16)


def _dot3(a, b):
    a_hi, a_lo = a
    b_hi, b_lo = b
    return _dot(a_hi, b_hi) + (_dot(a_lo, b_hi) + _dot(a_hi, b_lo))


def _in_proj_kernel(xp_ref, xs_ref, g_ref, w0, w1, w2, w3, w4,
                    glu_p, u_p, bg_p, glu_s, u_s, bg_s, whi_ref, wlo_ref, *, tc, n_pt):
    i = pl.program_id(1)

    @pl.when(i == 0)
    def _():
        for k, w in enumerate((w0, w1, w2, w3, w4)):
            hi, lo = _split(w[...])
            whi_ref[:, k * tc:(k + 1) * tc] = hi
            wlo_ref[:, k * tc:(k + 1) * tc] = lo

    def emit(proj, glu_ref, u_ref, bg_ref):
        glu_ref[...] = proj[:, 0:tc] * jax.nn.sigmoid(proj[:, tc:2 * tc])
        bg_ref[...] = proj[:, 2 * tc:3 * tc]
        u_ref[...] = proj[:, 3 * tc:4 * tc] * proj[:, 4 * tc:5 * tc]

    @pl.when(i < n_pt)
    def _():
        h = _rms(xp_ref[...], g_ref[...]).astype(BF16)
        emit(_dot(h, whi_ref[...]), glu_p, u_p, bg_p)

    @pl.when(i == n_pt)
    def _():
        h = _split(_rms(xs_ref[...], g_ref[...]))
        emit(_dot3(h, (whi_ref[...], wlo_ref[...])), glu_s, u_s, bg_s)


def _in_proj(x_p, x_s, gain, w_in, layer, d_half):
    n_p, d = x_p.shape
    n_s = x_s.shape[0]
    tm, tc = IN_ROWS, IN_COLS
    nct, n_pt = d_half // tc, n_p // tm
    w_specs = [
        pl.BlockSpec((None, d, tc), functools.partial(lambda c, i, k: (layer, 0, k * nct + c), k=k))
        for k in range(5)
    ]
    prow = lambda c, i: jnp.minimum(i, n_pt - 1)
    out_p = jax.ShapeDtypeStruct((n_p, d_half), F32)
    out_s = jax.ShapeDtypeStruct((n_s, d_half), F32)
    op_spec = pl.BlockSpec((tm, tc), lambda c, i: (prow(c, i), c))
    os_spec = pl.BlockSpec((n_s, tc), lambda c, i: (0, c))
    return pl.pallas_call(
        functools.partial(_in_proj_kernel, tc=tc, n_pt=n_pt),
        grid=(nct, n_pt + 1),
        in_specs=[pl.BlockSpec((tm, d), lambda c, i: (prow(c, i), 0)),
                  pl.BlockSpec((n_s, d), lambda c, i: (0, 0)),
                  pl.BlockSpec((None, 1, d), lambda c, i: (layer, 0, 0))] + w_specs,
        out_specs=[op_spec, op_spec, op_spec, os_spec, os_spec, os_spec],
        out_shape=[out_p, out_p, out_p, out_s, out_s, out_s],
        scratch_shapes=[pltpu.VMEM((d, 5 * tc), BF16), pltpu.VMEM((d, 5 * tc), BF16)],
        compiler_params=_params("arbitrary", "arbitrary"),
        name="in_proj",
    )(x_p, x_s, gain, w_in, w_in, w_in, w_in, w_in)


def _cast_kernel(w_ref, o_ref):
    o_ref[...] = w_ref[...].astype(o_ref.dtype)


def _to_bf16(w):
    l, r, c = w.shape
    tr = min(r, 512)
    spec = pl.BlockSpec((None, tr, c), lambda i, j: (i, j, 0))
    return pl.pallas_call(
        _cast_kernel, grid=(l, r // tr), in_specs=[spec], out_specs=spec,
        out_shape=jax.ShapeDtypeStruct(w.shape, BF16),
        compiler_params=_params("arbitrary", "arbitrary"), name="cast_bf16",
    )(w)


def _route(logits, idx_ref, gate_ref):
    n_e = logits.shape[-1]
    lane = lax.broadcasted_iota(jnp.int32, logits.shape, 1)
    m1 = jnp.max(logits, axis=-1, keepdims=True)
    i1 = jnp.min(jnp.where(logits == m1, lane, n_e), axis=-1, keepdims=True)
    rest = jnp.where(lane == i1, -jnp.inf, logits)
    m2 = jnp.max(rest, axis=-1, keepdims=True)
    i2 = jnp.min(jnp.where(rest == m2, lane, n_e), axis=-1, keepdims=True)
    e2 = jnp.exp(m2 - m1)
    den = 1.0 + e2
    col = lax.broadcasted_iota(jnp.int32, (logits.shape[0], TOP_K), 1)
    idx_ref[...] = jnp.where(col == 0, i1, i2)
    gate_ref[...] = jnp.where(col == 0, 1.0 / den, e2 / den)


def _ln_silu(conv, cb_ref, lg_ref, lb_ref):
    v = conv + cb_ref[...]
    mu = jnp.mean(v, axis=-1, keepdims=True)
    var = jnp.mean(jnp.square(v - mu), axis=-1, keepdims=True)
    z = (v - mu) * lax.rsqrt(var + LN_EPS) * lg_ref[...] + lb_ref[...]
    return z * jax.nn.sigmoid(z)


def _shift_phases(offset, taps):
    return sorted({(offset + k) % SUBLANES for k in range(taps)} - {0})


def _mixer_prompt_kernel(*refs, tm, ka, kb, halo_a, halo_b, routed, n_batch):
    (glu_ref, gluh_ref, u_ref, uh_ref, bg_ref, x_ref, caw_ref, cab_ref, lg_ref, lb_ref, cbw_ref,
     wout_ref) = refs[:12]
    refs = refs[12:]
    if routed:
        fg_ref, router_ref, x1_ref, h2_ref, idx_ref, gate_ref = refs[:6]
        refs = refs[6:]
    else:
        x1_ref = refs[0]
        refs = refs[1:]
    win_ref, sha_ref, uwin_ref, shb_ref, ca_ref, y_ref = refs
    b, t = pl.program_id(0), pl.program_id(1)
    da = glu_ref.shape[-1]

    def body():
        first = t == 0
        win_ref[0:halo_a, :] = jnp.where(first, 0.0, gluh_ref[...])
        win_ref[halo_a:halo_a + tm, :] = glu_ref[...]
        uwin_ref[0:halo_b, :] = jnp.where(first, 0.0, uh_ref[...])
        uwin_ref[halo_b:halo_b + tm, :] = u_ref[...]

        off_a = halo_a - (ka - 1)
        off_b = halo_b - (kb - 1)
        phases_a = _shift_phases(off_a, ka)
        phases_b = _shift_phases(off_b, kb)
        len_a = halo_a + tm - SUBLANES
        for j, p in enumerate(phases_a):
            sha_ref[j, 0:len_a, :] = win_ref[p:p + len_a, :]
        len_b = halo_b + tm - SUBLANES
        for j, p in enumerate(phases_b):
            shb_ref[j, 0:len_b, :] = uwin_ref[p:p + len_b, :]

        def tap(base_ref, shift_ref, phases, offset, r0, n_rows, cols):
            q, p = divmod(offset, SUBLANES)
            rows = pl.ds(r0 + q * SUBLANES, n_rows)
            return base_ref[rows, cols] if p == 0 else shift_ref[phases.index(p), rows, cols]

        strip_rows = CONV_STRIP_ROWS
        for lane0 in range(0, da, LANES):
            cols = slice(lane0, lane0 + LANES)
            taps = [jnp.broadcast_to(caw_ref[k:k + 1, cols], (SUBLANES, LANES)) for k in range(ka)]

            def strip(c, carry):
                r0 = pl.multiple_of(c * strip_rows, strip_rows)
                for g in range(0, strip_rows, SUBLANES):
                    acc = taps[0] * tap(win_ref, sha_ref, phases_a, off_a, r0 + g, SUBLANES, cols)
                    for k in range(1, ka):
                        acc = acc + taps[k] * tap(win_ref, sha_ref, phases_a, off_a + k, r0 + g,
                                                  SUBLANES, cols)
                    ca_ref[pl.ds(r0 + g, SUBLANES), cols] = acc
                return carry

            lax.fori_loop(0, tm // strip_rows, strip, 0)

        rb = CONV_CHUNK
        full = slice(None)

        def chunk(c, carry):
            r0 = pl.multiple_of(c * rb, rb)
            rows = pl.ds(r0, rb)
            y_ref[rows, 0:da] = _ln_silu(ca_ref[rows, :], cab_ref, lg_ref, lb_ref).astype(BF16)
            accb = jnp.zeros((rb, da), F32)
            for k in range(kb):
                accb = accb + cbw_ref[k:k + 1, :] * tap(uwin_ref, shb_ref, phases_b, off_b + k, r0,
                                                        rb, full)
            y_ref[rows, da:2 * da] = (bg_ref[rows, :] * accb).astype(BF16)
            return carry

        lax.fori_loop(0, tm // rb, chunk, 0)

        x1 = x_ref[...] + _dot(y_ref[...], wout_ref[...])
        x1_ref[...] = x1
        if routed:
            h2 = _rms(x1, fg_ref[...])
            h2_ref[...] = h2
            _route(_dot(h2.astype(BF16), router_ref[...].astype(BF16)), idx_ref, gate_ref)

    if not routed:
        body()
    else:
        pl.when(b < n_batch)(body)

        @pl.when((b == n_batch) & (t == 0))
        def _():
            h2_ref[...] = jnp.zeros_like(h2_ref)


def _mixer_prompt(glu, u, bg, x, caw, cab, lg, lb, cbw, w_out_bf, fgain, router, layer, batch, seq):
    d = x.shape[1]
    da = glu.shape[1]
    ka, kb = caw.shape[1], cbw.shape[1]
    tm = MIX_ROWS
    nt = seq // tm
    halo_a, halo_b = 32, SUBLANES
    assert ka - 1 <= halo_a and kb - 1 <= halo_b and seq % tm == 0 and tm % halo_a == 0
    routed = router is not None
    n_rows = batch * seq
    n_blocks = batch * nt

    def blk(b, t):
        return jnp.minimum(b * nt + t, n_blocks - 1)

    def row(b, t):
        return (blk(b, t), 0)

    def halo(width):
        per = tm // width
        return lambda b, t: (jnp.maximum(blk(b, t) * per - 1, 0), 0)

    vec = lambda n: pl.BlockSpec((None, 1, n), lambda b, t: (layer, 0, 0))
    in_specs = [
        pl.BlockSpec((tm, da), row), pl.BlockSpec((halo_a, da), halo(halo_a)),
        pl.BlockSpec((tm, da), row), pl.BlockSpec((halo_b, da), halo(halo_b)),
        pl.BlockSpec((tm, da), row), pl.BlockSpec((tm, d), row),
        pl.BlockSpec((None, ka, da), lambda b, t: (layer, 0, 0)), vec(da), vec(da), vec(da),
        pl.BlockSpec((None, kb, da), lambda b, t: (layer, 0, 0)),
        pl.BlockSpec((None, d, d), lambda b, t: (layer, 0, 0)),
    ]
    args = [glu, glu, u, u, bg, x, caw, cab, lg, lb, cbw, w_out_bf]
    out_shape = [jax.ShapeDtypeStruct((n_rows, d), F32)]
    out_specs = [pl.BlockSpec((tm, d), row)]
    if routed:
        n_e = router.shape[-1]
        in_specs += [vec(d), pl.BlockSpec((d, n_e), lambda b, t: (0, 0))]
        args += [fgain, router]
        out_shape += [jax.ShapeDtypeStruct((n_rows + tm, d), F32),
                      jax.ShapeDtypeStruct((n_rows, TOP_K), jnp.int32),
                      jax.ShapeDtypeStruct((n_rows, TOP_K), F32)]
        out_specs += [pl.BlockSpec((tm, d), lambda b, t: (jnp.minimum(b * nt + t, n_blocks), 0)),
                      pl.BlockSpec((tm, TOP_K), row), pl.BlockSpec((tm, TOP_K), row)]
    n_phase_a = len(_shift_phases(halo_a - (ka - 1), ka))
    n_phase_b = len(_shift_phases(halo_b - (kb - 1), kb))
    return pl.pallas_call(
        functools.partial(_mixer_prompt_kernel, tm=tm, ka=ka, kb=kb, halo_a=halo_a, halo_b=halo_b,
                          routed=routed, n_batch=batch),
        grid=(batch + 1 if routed else batch, nt),
        in_specs=in_specs, out_specs=out_specs, out_shape=out_shape,
        scratch_shapes=[pltpu.VMEM((halo_a + tm, da), F32),
                        pltpu.VMEM((n_phase_a, halo_a + tm, da), F32),
                        pltpu.VMEM((halo_b + tm, da), F32),
                        pltpu.VMEM((n_phase_b, halo_b + tm, da), F32),
                        pltpu.VMEM((tm, da), F32),
                        pltpu.VMEM((tm, d), BF16)],
        compiler_params=_params("arbitrary", "arbitrary"),
        name="mixer_prompt",
    )(*args)


def _mixer_sample_kernel(*refs, ka, kb, n_blk, n_k, routed):
    (glu_ref, u_ref, bg_ref, x_ref, sa_ref, ss_ref, caw_ref, cab_ref, lg_ref, lb_ref, cbw_ref,
     wout_ref) = refs[:12]
    refs = refs[12:]
    if routed:
        fg_ref, router_ref, _ = refs[:3]
        refs = refs[3:]
    x1_ref = refs[0]
    refs = refs[1:]
    if routed:
        h2_ref, idx_ref, gate_ref = refs[:3]
        refs = refs[3:]
    ca_ref, y_ref, acc_ref = refs
    s = pl.program_id(0)
    n_seq, kc = glu_ref.shape
    da = n_blk * kc

    @pl.when(s < n_blk)
    def _():
        conv = caw_ref[ka - 1:ka, :] * glu_ref[...]
        for k in range(ka - 1):
            conv = conv + caw_ref[k:k + 1, :] * sa_ref[k]
        ca_ref[s] = conv
        conv_b = cbw_ref[kb - 1:kb, :] * u_ref[...]
        for k in range(kb - 1):
            conv_b = conv_b + cbw_ref[k:k + 1, :] * ss_ref[k]
        y_ref[n_blk + s] = bg_ref[...] * conv_b

    @pl.when(s == n_blk - 1)
    def _():
        tiles = [ca_ref[c] + cab_ref[:, c * kc:(c + 1) * kc] for c in range(n_blk)]
        mu = sum(jnp.sum(v, axis=-1, keepdims=True) for v in tiles) / da
        var = sum(jnp.sum(jnp.square(v - mu), axis=-1, keepdims=True) for v in tiles) / da
        inv = lax.rsqrt(var + LN_EPS)
        for c in range(n_blk):
            cols = slice(c * kc, (c + 1) * kc)
            z = (tiles[c] - mu) * inv * lg_ref[:, cols] + lb_ref[:, cols]
            y_ref[c] = z * jax.nn.sigmoid(z)

    @pl.when(s == n_blk)
    def _():
        acc_ref[...] = x_ref[...]

    @pl.when(s >= n_blk)
    def _():
        acc_ref[...] += _dot3(_split(y_ref[s - n_blk]), _split(wout_ref[...]))

    @pl.when(s == n_blk + n_k - 1)
    def _():
        x1 = acc_ref[...]
        x1_ref[...] = x1
        if routed:
            h2 = _rms(x1, fg_ref[...])
            h2_ref[...] = h2
            _route(_dot3(_split(h2), _split(router_ref[...])), idx_ref, gate_ref)


def _mixer_sample(glu, u, bg, x, state_a_t, state_s_t, caw, cab, lg, lb, cbw, w_out, fgain, router,
                  h2_all, h2_row0, layer):
    n_seq, d = x.shape
    da = glu.shape[1]
    ka, kb = caw.shape[1], cbw.shape[1]
    kc = SAMPLE_K
    n_blk, n_k = da // kc, d // kc
    assert da % kc == 0 and d == 2 * da
    routed = router is not None
    full = lambda n: pl.BlockSpec((n_seq, n), lambda s: (0, 0))
    vec = lambda n: pl.BlockSpec((None, 1, n), lambda s: (layer, 0, 0))
    blk = lambda s: jnp.minimum(s, n_blk - 1)
    tile = pl.BlockSpec((n_seq, kc), lambda s: (0, blk(s)))
    in_specs = [
        tile, tile, tile, full(d),
        pl.BlockSpec((None, ka - 1, n_seq, kc), lambda s: (layer, 0, 0, blk(s))),
        pl.BlockSpec((None, kb - 1, n_seq, kc), lambda s: (layer, 0, 0, blk(s))),
        pl.BlockSpec((None, ka, kc), lambda s: (layer, 0, blk(s))), vec(da), vec(da), vec(da),
        pl.BlockSpec((None, kb, kc), lambda s: (layer, 0, blk(s))),
        pl.BlockSpec((None, kc, d), lambda s: (layer, jnp.maximum(s - n_blk, 0), 0)),
    ]
    args = [glu, u, bg, x, state_a_t, state_s_t, caw, cab, lg, lb, cbw, w_out]
    out_shape = [jax.ShapeDtypeStruct((n_seq, d), F32)]
    out_specs = [full(d)]
    aliases = {}
    if routed:
        n_e = router.shape[-1]
        assert h2_row0 % n_seq == 0 and h2_row0 + n_seq <= h2_all.shape[0]
        in_specs += [vec(d), pl.BlockSpec((d, n_e), lambda s: (0, 0)),
                     pl.BlockSpec(memory_space=pl.ANY)]
        args += [fgain, router, h2_all]
        aliases = {len(args) - 1: 1}
        out_shape += [jax.ShapeDtypeStruct(h2_all.shape, F32),
                      jax.ShapeDtypeStruct((n_seq, TOP_K), jnp.int32),
                      jax.ShapeDtypeStruct((n_seq, TOP_K), F32)]
        out_specs += [pl.BlockSpec((n_seq, d), lambda s: (h2_row0 // n_seq, 0)),
                      full(TOP_K), full(TOP_K)]
    return pl.pallas_call(
        functools.partial(_mixer_sample_kernel, ka=ka, kb=kb, n_blk=n_blk, n_k=n_k, routed=routed),
        grid=(n_blk + n_k,),
        in_specs=in_specs, out_specs=out_specs, out_shape=out_shape,
        input_output_aliases=aliases,
        scratch_shapes=[pltpu.VMEM((n_blk, n_seq, kc), F32),
                        pltpu.VMEM((n_k, n_seq, kc), F32),
                        pltpu.VMEM((n_seq, d), F32)],
        compiler_params=_params("arbitrary"),
        name="mixer_sample",
    )(*args)


def _state_update_kernel(sa_ref, ss_ref, glu_ref, u_ref, nsa_ref, nss_ref):
    for st_ref, new_ref, o_ref in ((sa_ref, glu_ref, nsa_ref), (ss_ref, u_ref, nss_ref)):
        n_hist = st_ref.shape[0]
        for k in range(n_hist - 1):
            o_ref[k] = st_ref[k + 1]
        o_ref[n_hist - 1] = new_ref[...]


def _state_update(state_a_t, state_s_t, glu_all, u_all):
    depth, n_a, n_seq, da = state_a_t.shape
    n_b = state_s_t.shape[1]
    kc = SAMPLE_K
    plane = lambda n: pl.BlockSpec((None, n, n_seq, kc), lambda l, c: (l, 0, 0, c))
    new = pl.BlockSpec((None, n_seq, kc), lambda l, c: (l, 0, c))
    return pl.pallas_call(
        _state_update_kernel,
        grid=(depth, da // kc),
        in_specs=[plane(n_a), plane(n_b), new, new],
        out_specs=[plane(n_a), plane(n_b)],
        out_shape=[jax.ShapeDtypeStruct(state_a_t.shape, F32),
                   jax.ShapeDtypeStruct(state_s_t.shape, F32)],
        compiler_params=_params("arbitrary", "arbitrary"),
        name="state_update",
    )(state_a_t, state_s_t, glu_all, u_all)


def _silu(g):
    return g * jax.nn.sigmoid(g)


def _dense_ffn_kernel(xp_ref, xs_ref, fg_ref, wg_ref, wu_ref, wd_ref, op_ref, os_ref,
                      h_ref, hs_hi_ref, hs_lo_ref, *, n_pt):
    i, f = pl.program_id(0), pl.program_id(1)

    @pl.when((f == 0) & (i < n_pt))
    def _():
        x = xp_ref[...]
        h_ref[...] = _rms(x, fg_ref[...]).astype(BF16)
        op_ref[...] = x

    @pl.when(i < n_pt)
    def _():
        h = h_ref[...]
        g = _dot(h, wg_ref[...].astype(BF16))
        up = _dot(h, wu_ref[...].astype(BF16))
        op_ref[...] += _dot((_silu(g) * up).astype(BF16), wd_ref[...].astype(BF16))

    @pl.when((f == 0) & (i == n_pt))
    def _():
        x = xs_ref[...]
        hs_hi_ref[...], hs_lo_ref[...] = _split(_rms(x, fg_ref[...]))
        os_ref[...] = x

    @pl.when(i == n_pt)
    def _():
        h = (hs_hi_ref[...], hs_lo_ref[...])
        g = _dot3(h, _split(wg_ref[...]))
        up = _dot3(h, _split(wu_ref[...]))
        os_ref[...] += _dot3(_split(_silu(g) * up), _split(wd_ref[...]))


def _dense_ffn(x_p, x_s, fgain, wg, wu, wd, layer, j):
    n_p, d = x_p.shape
    n_s = x_s.shape[0]
    ff = wg.shape[-1]
    tm, tf = DENSE_ROWS, FF_COLS
    n_pt = n_p // tm
    prow = lambda i, f: (jnp.minimum(i, n_pt - 1), 0)
    return pl.pallas_call(
        functools.partial(_dense_ffn_kernel, n_pt=n_pt),
        grid=(n_pt + 1, ff // tf),
        in_specs=[pl.BlockSpec((tm, d), prow, pipeline_mode=pl.Buffered(1)),
                  pl.BlockSpec((n_s, d), lambda i, f: (0, 0)),
                  pl.BlockSpec((None, 1, d), lambda i, f: (layer, 0, 0)),
                  pl.BlockSpec((None, d, tf), lambda i, f: (j, 0, f)),
                  pl.BlockSpec((None, d, tf), lambda i, f: (j, 0, f)),
                  pl.BlockSpec((None, tf, d), lambda i, f: (j, f, 0))],
        out_specs=[pl.BlockSpec((tm, d), prow), pl.BlockSpec((n_s, d), lambda i, f: (0, 0))],
        out_shape=[jax.ShapeDtypeStruct((n_p, d), F32), jax.ShapeDtypeStruct((n_s, d), F32)],
        scratch_shapes=[pltpu.VMEM((tm, d), BF16), pltpu.VMEM((n_s, d), BF16),
                        pltpu.VMEM((n_s, d), BF16)],
        compiler_params=_params("arbitrary", "arbitrary"),
        name="dense_ffn",
    )(x_p, x_s, fgain, wg, wu, wd)


def _row_copy(src_hbm, dst_ref, src_row, dst_row, sem):
    return pltpu.make_async_copy(src_hbm.at[pl.ds(src_row, 1)], dst_ref.at[pl.ds(dst_row, 1)], sem)


def _expert_ffn_kernel(rid_ref, te_ref, sa_ref, h_hbm, sg_ref, wg_ref, wu_ref, wd_ref,
                       o_ref, land_ref, xs_ref, wgb_ref, wub_ref, wdb_ref, sem, *, blocks, chunk):
    t, f = pl.program_id(0), pl.program_id(1)
    n_t, n_f = pl.num_programs(0), pl.num_programs(1)
    tm = xs_ref.shape[0]
    n_land = land_ref.shape[0]

    def fetch(tile, r):
        tok = rid_ref[tile * tm + jnp.minimum(r, tm - 1)]
        _row_copy(h_hbm, land_ref, tok, r, sem).start()

    def wait_landed():
        pltpu.make_async_copy(h_hbm.at[pl.ds(0, n_land)], land_ref, sem).wait()

    @pl.when(f == 0)
    def _():
        @pl.when(t == 0)
        def _():
            def body(r, carry):
                fetch(0, r)
                return carry

            lax.fori_loop(0, n_land, body, 0)

        wait_landed()
        xs_ref[...] = land_ref[0:tm, :].astype(BF16)
        o_ref[...] = jnp.zeros_like(o_ref)

    nxt = jnp.minimum(t + 1, n_t - 1)

    def prefetch():
        for r in range(chunk):
            fetch(nxt, f * chunk + r)

    def swiglu(rows, wg, wu, wd):
        x = xs_ref[rows, :]
        g = _dot(x, wg)
        up = _dot(x, wu)
        o_ref[rows, :] += _dot((_silu(g) * up).astype(BF16), wd)

    n_blk = len(blocks)
    starts = [sum(blocks[:b]) for b in range(n_blk)]
    active = [sa_ref[t * n_blk + b] == 1 for b in range(n_blk)]
    partial = jnp.logical_not(active[-1])

    @pl.when(active[-1])
    def _():
        prefetch()
        swiglu(slice(0, tm), wg_ref[...].astype(BF16), wu_ref[...].astype(BF16),
               wd_ref[...].astype(BF16))

    @pl.when(active[0] & partial)
    def _():
        wgb_ref[...] = wg_ref[...].astype(BF16)
        wub_ref[...] = wu_ref[...].astype(BF16)
        wdb_ref[...] = wd_ref[...].astype(BF16)
        prefetch()

    @pl.when(jnp.logical_not(active[0]))
    def _():
        prefetch()

    for b in range(n_blk - 1):
        @pl.when(active[b] & partial)
        def _():
            swiglu(slice(starts[b], starts[b] + blocks[b]), wgb_ref[...], wub_ref[...], wdb_ref[...])

    @pl.when(f == n_f - 1)
    def _():
        for b in range(n_blk):
            rows = slice(starts[b], starts[b] + blocks[b])

            @pl.when(active[b])
            def _():
                o_ref[rows, :] = sg_ref[rows, :] * o_ref[rows, :]

        @pl.when(t == n_t - 1)
        def _():
            wait_landed()


def _expert_ffn(row_ids, tile_expert, sub_active, h2, slot_gate, wg, wu, wd, j):
    s = row_ids.shape[0]
    d = h2.shape[1]
    ff = wg.shape[-1]
    blocks, tf = MOE_BLOCKS, MOE_FF_COLS
    tm = sum(blocks)
    nf = ff // tf
    assert s % tm == 0
    chunk = -(-tm // nf)

    def fcol(f, t, sa):
        busy = sa[t * len(blocks)]
        return f * busy + (nf - 1) * (1 - busy)

    grid_spec = pltpu.PrefetchScalarGridSpec(
        num_scalar_prefetch=3,
        grid=(s // tm, nf),
        in_specs=[pl.BlockSpec(memory_space=pl.ANY),
                  pl.BlockSpec((tm, 1), lambda t, f, rid, te, sa: (t, 0)),
                  pl.BlockSpec((None, None, d, tf),
                               lambda t, f, rid, te, sa: (j, te[t], 0, fcol(f, t, sa))),
                  pl.BlockSpec((None, None, d, tf),
                               lambda t, f, rid, te, sa: (j, te[t], 0, fcol(f, t, sa))),
                  pl.BlockSpec((None, None, tf, d),
                               lambda t, f, rid, te, sa: (j, te[t], fcol(f, t, sa), 0))],
        out_specs=pl.BlockSpec((tm, d), lambda t, f, rid, te, sa: (t, 0)),
        scratch_shapes=[pltpu.VMEM((chunk * nf, d), F32), pltpu.VMEM((tm, d), BF16),
                        pltpu.VMEM((d, tf), BF16), pltpu.VMEM((d, tf), BF16),
                        pltpu.VMEM((tf, d), BF16), pltpu.SemaphoreType.DMA(())],
    )
    return pl.pallas_call(
        functools.partial(_expert_ffn_kernel, blocks=blocks, chunk=chunk),
        grid_spec=grid_spec,
        out_shape=jax.ShapeDtypeStruct((s, d), F32),
        compiler_params=_params("arbitrary", "arbitrary"),
        name="expert_ffn",
    )(row_ids, tile_expert, sub_active, h2, slot_gate, wg, wu, wd)


def _combine_kernel(pos_ref, xp_ref, xs_ref, g_ref, ys_hbm, op_ref, os_ref, buf_ref, sems, *,
                    n_pt, final):
    i = pl.program_id(0)
    n_i = pl.num_programs(0)
    tk = xp_ref.shape[0]
    slot = i % 2

    def fetch(tile, dst_slot):
        for r in range(tk):
            for k in range(TOP_K):
                _row_copy(ys_hbm, buf_ref.at[dst_slot, k], pos_ref[(tile * tk + r) * TOP_K + k], r,
                          sems.at[dst_slot]).start()

    def wait(src_slot):
        for k in range(TOP_K):
            pltpu.make_async_copy(ys_hbm.at[pl.ds(0, tk)], buf_ref.at[src_slot, k],
                                  sems.at[src_slot]).wait()

    @pl.when(i == 0)
    def _():
        fetch(0, 0)

    fetch(jnp.minimum(i + 1, n_i - 1), 1 - slot)
    wait(slot)

    def emit(x_ref, o_ref):
        x2 = x_ref[...] + (buf_ref[slot, 0] + buf_ref[slot, 1])
        o_ref[...] = _rms(x2, g_ref[...]) if final else x2

    @pl.when(i == n_i - 1)
    def _():
        wait(1 - slot)

    @pl.when(i < n_pt)
    def _():
        emit(xp_ref, op_ref)

    @pl.when(i == n_pt)
    def _():
        emit(xs_ref, os_ref)


def _combine(pos, x_p, x_s, ys, final_gain, final):
    n_p, d = x_p.shape
    n_s = x_s.shape[0]
    tk = COMBINE_ROWS
    assert n_s == tk and n_p % tk == 0
    n_pt = n_p // tk
    prow = lambda i, pos: (jnp.minimum(i, n_pt - 1), 0)
    grid_spec = pltpu.PrefetchScalarGridSpec(
        num_scalar_prefetch=1,
        grid=(n_pt + 1,),
        in_specs=[pl.BlockSpec((tk, d), prow),
                  pl.BlockSpec((n_s, d), lambda i, pos: (0, 0)),
                  pl.BlockSpec((1, d), lambda i, pos: (0, 0)),
                  pl.BlockSpec(memory_space=pl.ANY)],
        out_specs=[pl.BlockSpec((tk, d), prow), pl.BlockSpec((n_s, d), lambda i, pos: (0, 0))],
        scratch_shapes=[pltpu.VMEM((2, TOP_K, tk, d), F32), pltpu.SemaphoreType.DMA((2,))],
    )
    return pl.pallas_call(
        functools.partial(_combine_kernel, n_pt=n_pt, final=final),
        grid_spec=grid_spec,
        out_shape=[jax.ShapeDtypeStruct((n_p, d), F32), jax.ShapeDtypeStruct((n_s, d), F32)],
        compiler_params=_params("arbitrary"),
        name="combine",
    )(pos, x_p, x_s, final_gain, ys)


def _final_norm_kernel(x_ref, g_ref, o_ref):
    o_ref[...] = _rms(x_ref[...], g_ref[...])


def _final_norm(x, gain):
    n, d = x.shape
    tm = min(n, IN_ROWS)
    return pl.pallas_call(
        _final_norm_kernel,
        grid=(n // tm,),
        in_specs=[pl.BlockSpec((tm, d), lambda i: (i, 0)), pl.BlockSpec((1, d), lambda i: (0, 0))],
        out_specs=pl.BlockSpec((tm, d), lambda i: (i, 0)),
        out_shape=jax.ShapeDtypeStruct((n, d), F32),
        compiler_params=_params("arbitrary"),
        name="final_norm",
    )(x, gain)


def _slot_layout(top_idx, gates, n_experts, blocks):
    tm = sum(blocks)
    n = top_idx.shape[0]
    m = n * TOP_K
    n_tiles = (m + n_experts * (tm - 1)) // tm
    s = n_tiles * tm
    flat_e = top_idx.reshape(m)
    flat_g = gates.reshape(m)
    onehot = (flat_e[:, None] == jnp.arange(n_experts, dtype=jnp.int32)[None, :]).astype(jnp.int32)
    counts = jnp.sum(onehot, axis=0)
    padded = ((counts + tm - 1) // tm) * tm
    pad_end = jnp.cumsum(padded)
    pad_start = pad_end - padded
    cnt_start = jnp.cumsum(counts) - counts
    pos = jnp.sum(onehot * (jnp.cumsum(onehot, axis=0) + pad_start[None, :]), axis=1) - 1
    tile_start = jnp.arange(n_tiles, dtype=jnp.int32) * tm
    tile_valid = tile_start < pad_end[-1]
    tile_expert = jnp.minimum(jnp.sum((pad_end[None, :] <= tile_start[:, None]).astype(jnp.int32), axis=1),
                              n_experts - 1)
    order = jnp.argsort(flat_e, stable=True).astype(jnp.int32)
    local = (tile_start - pad_start[tile_expert])[:, None] + jnp.arange(tm, dtype=jnp.int32)[None, :]
    live = (local < counts[tile_expert][:, None]) & tile_valid[:, None]
    src = order[jnp.clip(cnt_start[tile_expert][:, None] + local, 0, m - 1).reshape(s)]
    live = live.reshape(s)
    row_ids = jnp.where(live, src // TOP_K, 0).astype(jnp.int32)
    slot_gate = jnp.where(live, flat_g[src], 0.0).astype(F32)
    block_starts = [sum(blocks[:b]) for b in range(len(blocks))]
    sub_active = live.reshape(n_tiles, tm)[:, block_starts].reshape(-1).astype(jnp.int32)
    return row_ids, slot_gate[:, None], pos.astype(jnp.int32), tile_expert, sub_active


def kernel(x_prompt, x_sample, state_conv_a, state_short, mix_norm, w_in, conv_a_w, conv_a_b, ln_a_g, ln_a_b, conv_b_w, w_out, ffn_norm, dense_w_gate, dense_w_up, dense_w_down, router, exp_w_gate, exp_w_up, exp_w_down, final_norm):
    batch, seq, d = x_prompt.shape
    n_seq, dec_seq, _ = x_sample.shape
    depth = w_in.shape[0]
    da = conv_a_w.shape[-1]
    ka, kb = conv_a_w.shape[1], conv_b_w.shape[1]
    n_experts = router.shape[-1]
    assert dec_seq == 1 and conv_b_w.shape[-1] == da and w_in.shape[-1] == 5 * da
    assert seq >= ka - 1 and seq >= kb - 1
    n_p = batch * seq

    x_p = x_prompt.reshape(n_p, d)
    x_s = x_sample.reshape(n_seq, d)
    mix_gain = mix_norm.reshape(depth, 1, d)
    ffn_gain = ffn_norm.reshape(depth, 1, d)
    cab = conv_a_b.reshape(depth, 1, da)
    lg = ln_a_g.reshape(depth, 1, da)
    lb = ln_a_b.reshape(depth, 1, da)
    final_gain = final_norm.reshape(1, d)
    w_out_bf = _to_bf16(w_out)
    state_a_t = jnp.transpose(state_conv_a, (0, 2, 1, 3))
    state_s_t = jnp.transpose(state_short, (0, 2, 1, 3))

    new_a_p, new_s_p, glu_s_all, u_s_all = [], [], [], []
    normed = False
    for l in range(depth):
        glu_p, u_p, bg_p, glu_s, u_s, bg_s = _in_proj(x_p, x_s, mix_gain, w_in, l, da)
        new_a_p.append(glu_p.reshape(batch, seq, da)[:, seq - (ka - 1):])
        new_s_p.append(u_p.reshape(batch, seq, da)[:, seq - (kb - 1):])
        j = l // 2
        routed = l % 2 == 1
        rt = router[j] if routed else None
        outs_p = _mixer_prompt(glu_p, u_p, bg_p, x_p, conv_a_w, cab, lg, lb, conv_b_w, w_out_bf,
                               ffn_gain, rt, l, batch, seq)
        outs_s = _mixer_sample(glu_s, u_s, bg_s, x_s, state_a_t, state_s_t, conv_a_w, cab, lg,
                               lb, conv_b_w, w_out, ffn_gain, rt, outs_p[1] if routed else None,
                               n_p, l)
        glu_s_all.append(glu_s)
        u_s_all.append(u_s)
        x1_p, x1_s = outs_p[0], outs_s[0]
        if not routed:
            x_p, x_s = _dense_ffn(x1_p, x1_s, ffn_gain, dense_w_gate, dense_w_up, dense_w_down, l, j)
        else:
            top_idx = jnp.concatenate([outs_p[2], outs_s[2]], axis=0)
            gates = jnp.concatenate([outs_p[3], outs_s[3]], axis=0)
            h2 = outs_s[1]
            row_ids, slot_gate, pos, tile_expert, sub_active = _slot_layout(
                top_idx, gates, n_experts, MOE_BLOCKS)
            ys = _expert_ffn(row_ids, tile_expert, sub_active, h2, slot_gate,
                             exp_w_gate, exp_w_up, exp_w_down, j)
            normed = l == depth - 1
            x_p, x_s = _combine(pos, x1_p, x1_s, ys, final_gain, normed)
    if not normed:
        x_p, x_s = _final_norm(x_p, final_gain), _final_norm(x_s, final_gain)
    new_a_t, new_s_t = _state_update(state_a_t, state_s_t, jnp.stack(glu_s_all), jnp.stack(u_s_all))
    return (x_p.reshape(batch, seq, d), x_s.reshape(n_seq, 1, d), jnp.stack(new_a_p),
            jnp.stack(new_s_p), jnp.transpose(new_a_t, (0, 2, 1, 3)),
            jnp.transpose(new_s_t, (0, 2, 1, 3)))
```

```python
import functools

import jax
import jax.numpy as jnp
from jax import lax
from jax.experimental import pallas as pl
from jax.experimental.pallas import tpu as pltpu

F32 = jnp.float32
BF16 = jnp.bfloat16

RMS_EPS = 1e-6
LN_EPS = 1e-5
TOP_K = 2

SUBLANES = 8
LANES = 128
VMEM_LIMIT_BYTES = 56 * 1024 * 1024

IN_ROWS = 512
IN_COLS = 256
MIX_ROWS = 256
CONV_STRIP_ROWS = 64
CONV_CHUNK = 128
SAMPLE_K = 256
DENSE_ROWS = 1024
FF_COLS = 256
MOE_BLOCKS = (576, 576)
MOE_FF_COLS = 256
COMBINE_ROWS = 128


def _params(*sem):
    return pltpu.CompilerParams(dimension_semantics=sem, vmem_limit_bytes=VMEM_LIMIT_BYTES)


def _rms(x, gain):
    y = x * lax.rsqrt(jnp.mean(x * x, axis=-1, keepdims=True) + RMS_EPS)
    return y * gain


def _dot(a, b):
    return jnp.dot(a, b, preferred_element_type=F32)


def _split(v):
    hi = v.astype(BF16)
    return hi, (v - hi.astype(F32)).astype(BF16)


def _dot3(a, b):
    a_hi, a_lo = a
    b_hi, b_lo = b
    return _dot(a_hi, b_hi) + (_dot(a_lo, b_hi) + _dot(a_hi, b_lo))


def _in_proj_kernel(xp_ref, xs_ref, g_ref, w0, w1, w2, w3, w4,
                    glu_p, u_p, bg_p, glu_s, u_s, bg_s, whi_ref, wlo_ref, *, tc, n_pt):
    i = pl.program_id(1)

    @pl.when(i == 0)
    def _():
        for k, w in enumerate((w0, w1, w2, w3, w4)):
            hi, lo = _split(w[...])
            whi_ref[:, k * tc:(k + 1) * tc] = hi
            wlo_ref[:, k * tc:(k + 1) * tc] = lo

    def emit(proj, glu_ref, u_ref, bg_ref):
        glu_ref[...] = proj[:, 0:tc] * jax.nn.sigmoid(proj[:, tc:2 * tc])
        bg_ref[...] = proj[:, 2 * tc:3 * tc]
        u_ref[...] = proj[:, 3 * tc:4 * tc] * proj[:, 4 * tc:5 * tc]

    @pl.when(i < n_pt)
    def _():
        h = _rms(xp_ref[...], g_ref[...]).astype(BF16)
        emit(_dot(h, whi_ref[...]), glu_p, u_p, bg_p)

    @pl.when(i == n_pt)
    def _():
        h = _split(_rms(xs_ref[...], g_ref[...]))
        emit(_dot3(h, (whi_ref[...], wlo_ref[...])), glu_s, u_s, bg_s)


def _in_proj(x_p, x_s, gain, w_in, layer, d_half):
    n_p, d = x_p.shape
    n_s = x_s.shape[0]
    tm, tc = IN_ROWS, IN_COLS
    nct, n_pt = d_half // tc, n_p // tm
    w_specs = [
        pl.BlockSpec((None, d, tc), functools.partial(lambda c, i, k: (layer, 0, k * nct + c), k=k))
        for k in range(5)
    ]
    prow = lambda c, i: jnp.minimum(i, n_pt - 1)
    out_p = jax.ShapeDtypeStruct((n_p, d_half), F32)
    out_s = jax.ShapeDtypeStruct((n_s, d_half), F32)
    op_spec = pl.BlockSpec((tm, tc), lambda c, i: (prow(c, i), c))
    os_spec = pl.BlockSpec((n_s, tc), lambda c, i: (0, c))
    return pl.pallas_call(
        functools.partial(_in_proj_kernel, tc=tc, n_pt=n_pt),
        grid=(nct, n_pt + 1),
        in_specs=[pl.BlockSpec((tm, d), lambda c, i: (prow(c, i), 0)),
                  pl.BlockSpec((n_s, d), lambda c, i: (0, 0)),
                  pl.BlockSpec((None, 1, d), lambda c, i: (layer, 0, 0))] + w_specs,
        out_specs=[op_spec, op_spec, op_spec, os_spec, os_spec, os_spec],
        out_shape=[out_p, out_p, out_p, out_s, out_s, out_s],
        scratch_shapes=[pltpu.VMEM((d, 5 * tc), BF16), pltpu.VMEM((d, 5 * tc), BF16)],
        compiler_params=_params("arbitrary", "arbitrary"),
        name="in_proj",
    )(x_p, x_s, gain, w_in, w_in, w_in, w_in, w_in)


def _cast_kernel(w_ref, o_ref):
    o_ref[...] = w_ref[...].astype(o_ref.dtype)


def _to_bf16(w):
    l, r, c = w.shape
    tr = min(r, 512)
    spec = pl.BlockSpec((None, tr, c), lambda i, j: (i, j, 0))
    return pl.pallas_call(
        _cast_kernel, grid=(l, r // tr), in_specs=[spec], out_specs=spec,
        out_shape=jax.ShapeDtypeStruct(w.shape, BF16),
        compiler_params=_params("arbitrary", "arbitrary"), name="cast_bf16",
    )(w)


def _route(logits, idx_ref, gate_ref):
    n_e = logits.shape[-1]
    lane = lax.broadcasted_iota(jnp.int32, logits.shape, 1)
    m1 = jnp.max(logits, axis=-1, keepdims=True)
    i1 = jnp.min(jnp.where(logits == m1, lane, n_e), axis=-1, keepdims=True)
    rest = jnp.where(lane == i1, -jnp.inf, logits)
    m2 = jnp.max(rest, axis=-1, keepdims=True)
    i2 = jnp.min(jnp.where(rest == m2, lane, n_e), axis=-1, keepdims=True)
    e2 = jnp.exp(m2 - m1)
    den = 1.0 + e2
    col = lax.broadcasted_iota(jnp.int32, (logits.shape[0], TOP_K), 1)
    idx_ref[...] = jnp.where(col == 0, i1, i2)
    gate_ref[...] = jnp.where(col == 0, 1.0 / den, e2 / den)


def _ln_silu(conv, cb_ref, lg_ref, lb_ref):
    v = conv + cb_ref[...]
    mu = jnp.mean(v, axis=-1, keepdims=True)
    var = jnp.mean(jnp.square(v - mu), axis=-1, keepdims=True)
    z = (v - mu) * lax.rsqrt(var + LN_EPS) * lg_ref[...] + lb_ref[...]
    return z * jax.nn.sigmoid(z)


def _shift_phases(offset, taps):
    return sorted({(offset + k) % SUBLANES for k in range(taps)} - {0})


def _mixer_prompt_kernel(*refs, tm, ka, kb, halo_a, halo_b, routed, n_batch):
    (glu_ref, gluh_ref, u_ref, uh_ref, bg_ref, x_ref, caw_ref, cab_ref, lg_ref, lb_ref, cbw_ref,
     wout_ref) = refs[:12]
    refs = refs[12:]
    if routed:
        fg_ref, router_ref, x1_ref, h2_ref, idx_ref, gate_ref = refs[:6]
        refs = refs[6:]
    else:
        x1_ref = refs[0]
        refs = refs[1:]
    win_ref, sha_ref, uwin_ref, shb_ref, ca_ref, y_ref = refs
    b, t = pl.program_id(0), pl.program_id(1)
    da = glu_ref.shape[-1]

    def body():
        first = t == 0
        win_ref[0:halo_a, :] = jnp.where(first, 0.0, gluh_ref[...])
        win_ref[halo_a:halo_a + tm, :] = glu_ref[...]
        uwin_ref[0:halo_b, :] = jnp.where(first, 0.0, uh_ref[...])
        uwin_ref[halo_b:halo_b + tm, :] = u_ref[...]

        off_a = halo_a - (ka - 1)
        off_b = halo_b - (kb - 1)
        phases_a = _shift_phases(off_a, ka)
        phases_b = _shift_phases(off_b, kb)
        len_a = halo_a + tm - SUBLANES
        for j, p in enumerate(phases_a):
            sha_ref[j, 0:len_a, :] = win_ref[p:p + len_a, :]
        len_b = halo_b + tm - SUBLANES
        for j, p in enumerate(phases_b):
            shb_ref[j, 0:len_b, :] = uwin_ref[p:p + len_b, :]

        def tap(base_ref, shift_ref, phases, offset, r0, n_rows, cols):
            q, p = divmod(offset, SUBLANES)
            rows = pl.ds(r0 + q * SUBLANES, n_rows)
            return base_ref[rows, cols] if p == 0 else shift_ref[phases.index(p), rows, cols]

        strip_rows = CONV_STRIP_ROWS
        for lane0 in range(0, da, LANES):
            cols = slice(lane0, lane0 + LANES)
            taps = [jnp.broadcast_to(caw_ref[k:k + 1, cols], (SUBLANES, LANES)) for k in range(ka)]

            def strip(c, carry):
                r0 = pl.multiple_of(c * strip_rows, strip_rows)
                for g in range(0, strip_rows, SUBLANES):
                    acc = taps[0] * tap(win_ref, sha_ref, phases_a, off_a, r0 + g, SUBLANES, cols)
                    for k in range(1, ka):
                        acc = acc + taps[k] * tap(win_ref, sha_ref, phases_a, off_a + k, r0 + g,
                                                  SUBLANES, cols)
                    ca_ref[pl.ds(r0 + g, SUBLANES), cols] = acc
                return carry

            lax.fori_loop(0, tm // strip_rows, strip, 0)

        rb = CONV_CHUNK
        full = slice(None)

        def chunk(c, carry):
            r0 = pl.multiple_of(c * rb, rb)
            rows = pl.ds(r0, rb)
            y_ref[rows, 0:da] = _ln_silu(ca_ref[rows, :], cab_ref, lg_ref, lb_ref).astype(BF16)
            accb = jnp.zeros((rb, da), F32)
            for k in range(kb):
                accb = accb + cbw_ref[k:k + 1, :] * tap(uwin_ref, shb_ref, phases_b, off_b + k, r0,
                                                        rb, full)
            y_ref[rows, da:2 * da] = (bg_ref[rows, :] * accb).astype(BF16)
            return carry

        lax.fori_loop(0, tm // rb, chunk, 0)

        x1 = x_ref[...] + _dot(y_ref[...], wout_ref[...])
        x1_ref[...] = x1
        if routed:
            h2 = _rms(x1, fg_ref[...])
            h2_ref[...] = h2
            _route(_dot(h2.astype(BF16), router_ref[...].astype(BF16)), idx_ref, gate_ref)

    if not routed:
        body()
    else:
        pl.when(b < n_batch)(body)

        @pl.when((b == n_batch) & (t == 0))
        def _():
            h2_ref[...] = jnp.zeros_like(h2_ref)


def _mixer_prompt(glu, u, bg, x, caw, cab, lg, lb, cbw, w_out_bf, fgain, router, layer, batch, seq):
    d = x.shape[1]
    da = glu.shape[1]
    ka, kb = caw.shape[1], cbw.shape[1]
    tm = MIX_ROWS
    nt = seq // tm
    halo_a, halo_b = 32, SUBLANES
    assert ka - 1 <= halo_a and kb - 1 <= halo_b and seq % tm == 0 and tm % halo_a == 0
    routed = router is not None
    n_rows = batch * seq
    n_blocks = batch * nt

    def blk(b, t):
        return jnp.minimum(b * nt + t, n_blocks - 1)

    def row(b, t):
        return (blk(b, t), 0)

    def halo(width):
        per = tm // width
        return lambda b, t: (jnp.maximum(blk(b, t) * per - 1, 0), 0)

    vec = lambda n: pl.BlockSpec((None, 1, n), lambda b, t: (layer, 0, 0))
    in_specs = [
        pl.BlockSpec((tm, da), row), pl.BlockSpec((halo_a, da), halo(halo_a)),
        pl.BlockSpec((tm, da), row), pl.BlockSpec((halo_b, da), halo(halo_b)),
        pl.BlockSpec((tm, da), row), pl.BlockSpec((tm, d), row),
        pl.BlockSpec((None, ka, da), lambda b, t: (layer, 0, 0)), vec(da), vec(da), vec(da),
        pl.BlockSpec((None, kb, da), lambda b, t: (layer, 0, 0)),
        pl.BlockSpec((None, d, d), lambda b, t: (layer, 0, 0)),
    ]
    args = [glu, glu, u, u, bg, x, caw, cab, lg, lb, cbw, w_out_bf]
    out_shape = [jax.ShapeDtypeStruct((n_rows, d), F32)]
    out_specs = [pl.BlockSpec((tm, d), row)]
    if routed:
        n_e = router.shape[-1]
        in_specs += [vec(d), pl.BlockSpec((d, n_e), lambda b, t: (0, 0))]
        args += [fgain, router]
        out_shape += [jax.ShapeDtypeStruct((n_rows + tm, d), F32),
                      jax.ShapeDtypeStruct((n_rows, TOP_K), jnp.int32),
                      jax.ShapeDtypeStruct((n_rows, TOP_K), F32)]
        out_specs += [pl.BlockSpec((tm, d), lambda b, t: (jnp.minimum(b * nt + t, n_blocks), 0)),
                      pl.BlockSpec((tm, TOP_K), row), pl.BlockSpec((tm, TOP_K), row)]
    n_phase_a = len(_shift_phases(halo_a - (ka - 1), ka))
    n_phase_b = len(_shift_phases(halo_b - (kb - 1), kb))
    return pl.pallas_call(
        functools.partial(_mixer_prompt_kernel, tm=tm, ka=ka, kb=kb, halo_a=halo_a, halo_b=halo_b,
                          routed=routed, n_batch=batch),
        grid=(batch + 1 if routed else batch, nt),
        in_specs=in_specs, out_specs=out_specs, out_shape=out_shape,
        scratch_shapes=[pltpu.VMEM((halo_a + tm, da), F32),
                        pltpu.VMEM((n_phase_a, halo_a + tm, da), F32),
                        pltpu.VMEM((halo_b + tm, da), F32),
                        pltpu.VMEM((n_phase_b, halo_b + tm, da), F32),
                        pltpu.VMEM((tm, da), F32),
                        pltpu.VMEM((tm, d), BF16)],
        compiler_params=_params("arbitrary", "arbitrary"),
        name="mixer_prompt",
    )(*args)


def _mixer_sample_kernel(*refs, ka, kb, n_blk, n_k, routed):
    (glu_ref, u_ref, bg_ref, x_ref, sa_ref, ss_ref, caw_ref, cab_ref, lg_ref, lb_ref, cbw_ref,
     wout_ref) = refs[:12]
    refs = refs[12:]
    if routed:
        fg_ref, router_ref, _ = refs[:3]
        refs = refs[3:]
    x1_ref = refs[0]
    refs = refs[1:]
    if routed:
        h2_ref, idx_ref, gate_ref = refs[:3]
        refs = refs[3:]
    ca_ref, y_ref, acc_ref = refs
    s = pl.program_id(0)
    n_seq, kc = glu_ref.shape
    da = n_blk * kc

    @pl.when(s < n_blk)
    def _():
        conv = caw_ref[ka - 1:ka, :] * glu_ref[...]
        for k in range(ka - 1):
            conv = conv + caw_ref[k:k + 1, :] * sa_ref[k]
        ca_ref[s] = conv
        conv_b = cbw_ref[kb - 1:kb, :] * u_ref[...]
        for k in range(kb - 1):
            conv_b = conv_b + cbw_ref[k:k + 1, :] * ss_ref[k]
        y_ref[n_blk + s] = bg_ref[...] * conv_b

    @pl.when(s == n_blk - 1)
    def _():
        tiles = [ca_ref[c] + cab_ref[:, c * kc:(c + 1) * kc] for c in range(n_blk)]
        mu = sum(jnp.sum(v, axis=-1, keepdims=True) for v in tiles) / da
        var = sum(jnp.sum(jnp.square(v - mu), axis=-1, keepdims=True) for v in tiles) / da
        inv = lax.rsqrt(var + LN_EPS)
        for c in range(n_blk):
            cols = slice(c * kc, (c + 1) * kc)
            z = (tiles[c] - mu) * inv * lg_ref[:, cols] + lb_ref[:, cols]
            y_ref[c] = z * jax.nn.sigmoid(z)

    @pl.when(s == n_blk)
    def _():
        acc_ref[...] = x_ref[...]

    @pl.when(s >= n_blk)
    def _():
        acc_ref[...] += _dot3(_split(y_ref[s - n_blk]), _split(wout_ref[...]))

    @pl.when(s == n_blk + n_k - 1)
    def _():
        x1 = acc_ref[...]
        x1_ref[...] = x1
        if routed:
            h2 = _rms(x1, fg_ref[...])
            h2_ref[...] = h2
            _route(_dot3(_split(h2), _split(router_ref[...])), idx_ref, gate_ref)


def _mixer_sample(glu, u, bg, x, state_a_t, state_s_t, caw, cab, lg, lb, cbw, w_out, fgain, router,
                  h2_all, h2_row0, layer):
    n_seq, d = x.shape
    da = glu.shape[1]
    ka, kb = caw.shape[1], cbw.shape[1]
    kc = SAMPLE_K
    n_blk, n_k = da // kc, d // kc
    assert da % kc == 0 and d == 2 * da
    routed = router is not None
    full = lambda n: pl.BlockSpec((n_seq, n), lambda s: (0, 0))
    vec = lambda n: pl.BlockSpec((None, 1, n), lambda s: (layer, 0, 0))
    blk = lambda s: jnp.minimum(s, n_blk - 1)
    tile = pl.BlockSpec((n_seq, kc), lambda s: (0, blk(s)))
    in_specs = [
        tile, tile, tile, full(d),
        pl.BlockSpec((None, ka - 1, n_seq, kc), lambda s: (layer, 0, 0, blk(s))),
        pl.BlockSpec((None, kb - 1, n_seq, kc), lambda s: (layer, 0, 0, blk(s))),
        pl.BlockSpec((None, ka, kc), lambda s: (layer, 0, blk(s))), vec(da), vec(da), vec(da),
        pl.BlockSpec((None, kb, kc), lambda s: (layer, 0, blk(s))),
        pl.BlockSpec((None, kc, d), lambda s: (layer, jnp.maximum(s - n_blk, 0), 0)),
    ]
    args = [glu, u, bg, x, state_a_t, state_s_t, caw, cab, lg, lb, cbw, w_out]
    out_shape = [jax.ShapeDtypeStruct((n_seq, d), F32)]
    out_specs = [full(d)]
    aliases = {}
    if routed:
        n_e = router.shape[-1]
        assert h2_row0 % n_seq == 0 and h2_row0 + n_seq <= h2_all.shape[0]
        in_specs += [vec(d), pl.BlockSpec((d, n_e), lambda s: (0, 0)),
                     pl.BlockSpec(memory_space=pl.ANY)]
        args += [fgain, router, h2_all]
        aliases = {len(args) - 1: 1}
        out_shape += [jax.ShapeDtypeStruct(h2_all.shape, F32),
                      jax.ShapeDtypeStruct((n_seq, TOP_K), jnp.int32),
                      jax.ShapeDtypeStruct((n_seq, TOP_K), F32)]
        out_specs += [pl.BlockSpec((n_seq, d), lambda s: (h2_row0 // n_seq, 0)),
                      full(TOP_K), full(TOP_K)]
    return pl.pallas_call(
        functools.partial(_mixer_sample_kernel, ka=ka, kb=kb, n_blk=n_blk, n_k=n_k, routed=routed),
        grid=(n_blk + n_k,),
        in_specs=in_specs, out_specs=out_specs, out_shape=out_shape,
        input_output_aliases=aliases,
        scratch_shapes=[pltpu.VMEM((n_blk, n_seq, kc), F32),
                        pltpu.VMEM((n_k, n_seq, kc), F32),
                        pltpu.VMEM((n_seq, d), F32)],
        compiler_params=_params("arbitrary"),
        name="mixer_sample",
    )(*args)


def _state_update_kernel(sa_ref, ss_ref, glu_ref, u_ref, nsa_ref, nss_ref):
    for st_ref, new_ref, o_ref in ((sa_ref, glu_ref, nsa_ref), (ss_ref, u_ref, nss_ref)):
        n_hist = st_ref.shape[0]
        for k in range(n_hist - 1):
            o_ref[k] = st_ref[k + 1]
        o_ref[n_hist - 1] = new_ref[...]


def _state_update(state_a_t, state_s_t, glu_all, u_all):
    depth, n_a, n_seq, da = state_a_t.shape
    n_b = state_s_t.shape[1]
    kc = SAMPLE_K
    plane = lambda n: pl.BlockSpec((None, n, n_seq, kc), lambda l, c: (l, 0, 0, c))
    new = pl.BlockSpec((None, n_seq, kc), lambda l, c: (l, 0, c))
    return pl.pallas_call(
        _state_update_kernel,
        grid=(depth, da // kc),
        in_specs=[plane(n_a), plane(n_b), new, new],
        out_specs=[plane(n_a), plane(n_b)],
        out_shape=[jax.ShapeDtypeStruct(state_a_t.shape, F32),
                   jax.ShapeDtypeStruct(state_s_t.shape, F32)],
        compiler_params=_params("arbitrary", "arbitrary"),
        name="state_update",
    )(state_a_t, state_s_t, glu_all, u_all)


def _silu(g):
    return g * jax.nn.sigmoid(g)


def _dense_ffn_kernel(xp_ref, xs_ref, fg_ref, wg_ref, wu_ref, wd_ref, op_ref, os_ref,
                      h_ref, hs_hi_ref, hs_lo_ref, *, n_pt):
    i, f = pl.program_id(0), pl.program_id(1)

    @pl.when((f == 0) & (i < n_pt))
    def _():
        x = xp_ref[...]
        h_ref[...] = _rms(x, fg_ref[...]).astype(BF16)
        op_ref[...] = x

    @pl.when(i < n_pt)
    def _():
        h = h_ref[...]
        g = _dot(h, wg_ref[...].astype(BF16))
        up = _dot(h, wu_ref[...].astype(BF16))
        op_ref[...] += _dot((_silu(g) * up).astype(BF16), wd_ref[...].astype(BF16))

    @pl.when((f == 0) & (i == n_pt))
    def _():
        x = xs_ref[...]
        hs_hi_ref[...], hs_lo_ref[...] = _split(_rms(x, fg_ref[...]))
        os_ref[...] = x

    @pl.when(i == n_pt)
    def _():
        h = (hs_hi_ref[...], hs_lo_ref[...])
        g = _dot3(h, _split(wg_ref[...]))
        up = _dot3(h, _split(wu_ref[...]))
        os_ref[...] += _dot3(_split(_silu(g) * up), _split(wd_ref[...]))


def _dense_ffn(x_p, x_s, fgain, wg, wu, wd, layer, j):
    n_p, d = x_p.shape
    n_s = x_s.shape[0]
    ff = wg.shape[-1]
    tm, tf = DENSE_ROWS, FF_COLS
    n_pt = n_p // tm
    prow = lambda i, f: (jnp.minimum(i, n_pt - 1), 0)
    return pl.pallas_call(
        functools.partial(_dense_ffn_kernel, n_pt=n_pt),
        grid=(n_pt + 1, ff // tf),
        in_specs=[pl.BlockSpec((tm, d), prow, pipeline_mode=pl.Buffered(1)),
                  pl.BlockSpec((n_s, d), lambda i, f: (0, 0)),
                  pl.BlockSpec((None, 1, d), lambda i, f: (layer, 0, 0)),
                  pl.BlockSpec((None, d, tf), lambda i, f: (j, 0, f)),
                  pl.BlockSpec((None, d, tf), lambda i, f: (j, 0, f)),
                  pl.BlockSpec((None, tf, d), lambda i, f: (j, f, 0))],
        out_specs=[pl.BlockSpec((tm, d), prow), pl.BlockSpec((n_s, d), lambda i, f: (0, 0))],
        out_shape=[jax.ShapeDtypeStruct((n_p, d), F32), jax.ShapeDtypeStruct((n_s, d), F32)],
        scratch_shapes=[pltpu.VMEM((tm, d), BF16), pltpu.VMEM((n_s, d), BF16),
                        pltpu.VMEM((n_s, d), BF16)],
        compiler_params=_params("arbitrary", "arbitrary"),
        name="dense_ffn",
    )(x_p, x_s, fgain, wg, wu, wd)


def _row_copy(src_hbm, dst_ref, src_row, dst_row, sem):
    return pltpu.make_async_copy(src_hbm.at[pl.ds(src_row, 1)], dst_ref.at[pl.ds(dst_row, 1)], sem)


def _expert_ffn_kernel(rid_ref, te_ref, sa_ref, h_hbm, sg_ref, wg_ref, wu_ref, wd_ref,
                       o_ref, land_ref, xs_ref, wgb_ref, wub_ref, wdb_ref, sem, *, blocks, chunk):
    t, f = pl.program_id(0), pl.program_id(1)
    n_t, n_f = pl.num_programs(0), pl.num_programs(1)
    tm = xs_ref.shape[0]
    n_land = land_ref.shape[0]

    def fetch(tile, r):
        tok = rid_ref[tile * tm + jnp.minimum(r, tm - 1)]
        _row_copy(h_hbm, land_ref, tok, r, sem).start()

    def wait_landed():
        pltpu.make_async_copy(h_hbm.at[pl.ds(0, n_land)], land_ref, sem).wait()

    @pl.when(f == 0)
    def _():
        @pl.when(t == 0)
        def _():
            def body(r, carry):
                fetch(0, r)
                return carry

            lax.fori_loop(0, n_land, body, 0)

        wait_landed()
        xs_ref[...] = land_ref[0:tm, :].astype(BF16)
        o_ref[...] = jnp.zeros_like(o_ref)

    nxt = jnp.minimum(t + 1, n_t - 1)

    def prefetch():
        for r in range(chunk):
            fetch(nxt, f * chunk + r)

    def swiglu(rows, wg, wu, wd):
        x = xs_ref[rows, :]
        g = _dot(x, wg)
        up = _dot(x, wu)
        o_ref[rows, :] += _dot((_silu(g) * up).astype(BF16), wd)

    n_blk = len(blocks)
    starts = [sum(blocks[:b]) for b in range(n_blk)]
    active = [sa_ref[t * n_blk + b] == 1 for b in range(n_blk)]
    partial = jnp.logical_not(active[-1])

    @pl.when(active[-1])
    def _():
        prefetch()
        swiglu(slice(0, tm), wg_ref[...].astype(BF16), wu_ref[...].astype(BF16),
               wd_ref[...].astype(BF16))

    @pl.when(active[0] & partial)
    def _():
        wgb_ref[...] = wg_ref[...].astype(BF16)
        wub_ref[...] = wu_ref[...].astype(BF16)
        wdb_ref[...] = wd_ref[...].astype(BF16)
        prefetch()

    @pl.when(jnp.logical_not(active[0]))
    def _():
        prefetch()

    for b in range(n_blk - 1):
        @pl.when(active[b] & partial)
        def _():
            swiglu(slice(starts[b], starts[b] + blocks[b]), wgb_ref[...], wub_ref[...], wdb_ref[...])

    @pl.when(f == n_f - 1)
    def _():
        for b in range(n_blk):
            rows = slice(starts[b], starts[b] + blocks[b])

            @pl.when(active[b])
            def _():
                o_ref[rows, :] = sg_ref[rows, :] * o_ref[rows, :]

        @pl.when(t == n_t - 1)
        def _():
            wait_landed()


def _expert_ffn(row_ids, tile_expert, sub_active, h2, slot_gate, wg, wu, wd, j):
    s = row_ids.shape[0]
    d = h2.shape[1]
    ff = wg.shape[-1]
    blocks, tf = MOE_BLOCKS, MOE_FF_COLS
    tm = sum(blocks)
    nf = ff // tf
    assert s % tm == 0
    chunk = -(-tm // nf)

    def fcol(f, t, sa):
        busy = sa[t * len(blocks)]
        return f * busy + (nf - 1) * (1 - busy)

    grid_spec = pltpu.PrefetchScalarGridSpec(
        num_scalar_prefetch=3,
        grid=(s // tm, nf),
        in_specs=[pl.BlockSpec(memory_space=pl.ANY),
                  pl.BlockSpec((tm, 1), lambda t, f, rid, te, sa: (t, 0)),
                  pl.BlockSpec((None, None, d, tf),
                               lambda t, f, rid, te, sa: (j, te[t], 0, fcol(f, t, sa))),
                  pl.BlockSpec((None, None, d, tf),
                               lambda t, f, rid, te, sa: (j, te[t], 0, fcol(f, t, sa))),
                  pl.BlockSpec((None, None, tf, d),
                               lambda t, f, rid, te, sa: (j, te[t], fcol(f, t, sa), 0))],
        out_specs=pl.BlockSpec((tm, d), lambda t, f, rid, te, sa: (t, 0)),
        scratch_shapes=[pltpu.VMEM((chunk * nf, d), F32), pltpu.VMEM((tm, d), BF16),
                        pltpu.VMEM((d, tf), BF16), pltpu.VMEM((d, tf), BF16),
                        pltpu.VMEM((tf, d), BF16), pltpu.SemaphoreType.DMA(())],
    )
    return pl.pallas_call(
        functools.partial(_expert_ffn_kernel, blocks=blocks, chunk=chunk),
        grid_spec=grid_spec,
        out_shape=jax.ShapeDtypeStruct((s, d), F32),
        compiler_params=_params("arbitrary", "arbitrary"),
        name="expert_ffn",
    )(row_ids, tile_expert, sub_active, h2, slot_gate, wg, wu, wd)


def _combine_kernel(pos_ref, xp_ref, xs_ref, g_ref, ys_hbm, op_ref, os_ref, buf_ref, sems, *,
                    n_pt, final):
    i = pl.program_id(0)
    n_i = pl.num_programs(0)
    tk = xp_ref.shape[0]
    slot = i % 2

    def fetch(tile, dst_slot):
        for r in range(tk):
            for k in range(TOP_K):
                _row_copy(ys_hbm, buf_ref.at[dst_slot, k], pos_ref[(tile * tk + r) * TOP_K + k], r,
                          sems.at[dst_slot]).start()

    def wait(src_slot):
        for k in range(TOP_K):
            pltpu.make_async_copy(ys_hbm.at[pl.ds(0, tk)], buf_ref.at[src_slot, k],
                                  sems.at[src_slot]).wait()

    @pl.when(i == 0)
    def _():
        fetch(0, 0)

    fetch(jnp.minimum(i + 1, n_i - 1), 1 - slot)
    wait(slot)

    def emit(x_ref, o_ref):
        x2 = x_ref[...] + (buf_ref[slot, 0] + buf_ref[slot, 1])
        o_ref[...] = _rms(x2, g_ref[...]) if final else x2

    @pl.when(i == n_i - 1)
    def _():
        wait(1 - slot)

    @pl.when(i < n_pt)
    def _():
        emit(xp_ref, op_ref)

    @pl.when(i == n_pt)
    def _():
        emit(xs_ref, os_ref)


def _combine(pos, x_p, x_s, ys, final_gain, final):
    n_p, d = x_p.shape
    n_s = x_s.shape[0]
    tk = COMBINE_ROWS
    assert n_s == tk and n_p % tk == 0
    n_pt = n_p // tk
    prow = lambda i, pos: (jnp.minimum(i, n_pt - 1), 0)
    grid_spec = pltpu.PrefetchScalarGridSpec(
        num_scalar_prefetch=1,
        grid=(n_pt + 1,),
        in_specs=[pl.BlockSpec((tk, d), prow),
                  pl.BlockSpec((n_s, d), lambda i, pos: (0, 0)),
                  pl.BlockSpec((1, d), lambda i, pos: (0, 0)),
                  pl.BlockSpec(memory_space=pl.ANY)],
        out_specs=[pl.BlockSpec((tk, d), prow), pl.BlockSpec((n_s, d), lambda i, pos: (0, 0))],
        scratch_shapes=[pltpu.VMEM((2, TOP_K, tk, d), F32), pltpu.SemaphoreType.DMA((2,))],
    )
    return pl.pallas_call(
        functools.partial(_combine_kernel, n_pt=n_pt, final=final),
        grid_spec=grid_spec,
        out_shape=[jax.ShapeDtypeStruct((n_p, d), F32), jax.ShapeDtypeStruct((n_s, d), F32)],
        compiler_params=_params("arbitrary"),
        name="combine",
    )(pos, x_p, x_s, final_gain, ys)


def _final_norm_kernel(x_ref, g_ref, o_ref):
    o_ref[...] = _rms(x_ref[...], g_ref[...])


def _final_norm(x, gain):
    n, d = x.shape
    tm = min(n, IN_ROWS)
    return pl.pallas_call(
        _final_norm_kernel,
        grid=(n // tm,),
        in_specs=[pl.BlockSpec((tm, d), lambda i: (i, 0)), pl.BlockSpec((1, d), lambda i: (0, 0))],
        out_specs=pl.BlockSpec((tm, d), lambda i: (i, 0)),
        out_shape=jax.ShapeDtypeStruct((n, d), F32),
        compiler_params=_params("arbitrary"),
        name="final_norm",
    )(x, gain)


def _slot_layout(top_idx, gates, n_experts, blocks):
    tm = sum(blocks)
    n = top_idx.shape[0]
    m = n * TOP_K
    n_tiles = (m + n_experts * (tm - 1)) // tm
    s = n_tiles * tm
    flat_e = top_idx.reshape(m)
    flat_g = gates.reshape(m)
    onehot = (flat_e[:, None] == jnp.arange(n_experts, dtype=jnp.int32)[None, :]).astype(jnp.int32)
    counts = jnp.sum(onehot, axis=0)
    padded = ((counts + tm - 1) // tm) * tm
    pad_end = jnp.cumsum(padded)
    pad_start = pad_end - padded
    cnt_start = jnp.cumsum(counts) - counts
    pos = jnp.sum(onehot * (jnp.cumsum(onehot, axis=0) + pad_start[None, :]), axis=1) - 1
    tile_start = jnp.arange(n_tiles, dtype=jnp.int32) * tm
    tile_valid = tile_start < pad_end[-1]
    tile_expert = jnp.minimum(jnp.sum((pad_end[None, :] <= tile_start[:, None]).astype(jnp.int32), axis=1),
                              n_experts - 1)
    order = jnp.argsort(flat_e, stable=True).astype(jnp.int32)
    local = (tile_start - pad_start[tile_expert])[:, None] + jnp.arange(tm, dtype=jnp.int32)[None, :]
    live = (local < counts[tile_expert][:, None]) & tile_valid[:, None]
    src = order[jnp.clip(cnt_start[tile_expert][:, None] + local, 0, m - 1).reshape(s)]
    live = live.reshape(s)
    row_ids = jnp.where(live, src // TOP_K, 0).astype(jnp.int32)
    slot_gate = jnp.where(live, flat_g[src], 0.0).astype(F32)
    block_starts = [sum(blocks[:b]) for b in range(len(blocks))]
    sub_active = live.reshape(n_tiles, tm)[:, block_starts].reshape(-1).astype(jnp.int32)
    return row_ids, slot_gate[:, None], pos.astype(jnp.int32), tile_expert, sub_active


def kernel(x_prompt, x_sample, state_conv_a, state_short, mix_norm, w_in, conv_a_w, conv_a_b, ln_a_g, ln_a_b, conv_b_w, w_out, ffn_norm, dense_w_gate, dense_w_up, dense_w_down, router, exp_w_gate, exp_w_up, exp_w_down, final_norm):
    batch, seq, d = x_prompt.shape
    n_seq, dec_seq, _ = x_sample.shape
    depth = w_in.shape[0]
    da = conv_a_w.shape[-1]
    ka, kb = conv_a_w.shape[1], conv_b_w.shape[1]
    n_experts = router.shape[-1]
    assert dec_seq == 1 and conv_b_w.shape[-1] == da and w_in.shape[-1] == 5 * da
    assert seq >= ka - 1 and seq >= kb - 1
    n_p = batch * seq

    x_p = x_prompt.reshape(n_p, d)
    x_s = x_sample.reshape(n_seq, d)
    mix_gain = mix_norm.reshape(depth, 1, d)
    ffn_gain = ffn_norm.reshape(depth, 1, d)
    cab = conv_a_b.reshape(depth, 1, da)
    lg = ln_a_g.reshape(depth, 1, da)
    lb = ln_a_b.reshape(depth, 1, da)
    final_gain = final_norm.reshape(1, d)
    w_out_bf = _to_bf16(w_out)
    state_a_t = jnp.transpose(state_conv_a, (0, 2, 1, 3))
    state_s_t = jnp.transpose(state_short, (0, 2, 1, 3))

    new_a_p, new_s_p, glu_s_all, u_s_all = [], [], [], []
    normed = False
    for l in range(depth):
        glu_p, u_p, bg_p, glu_s, u_s, bg_s = _in_proj(x_p, x_s, mix_gain, w_in, l, da)
        new_a_p.append(glu_p.reshape(batch, seq, da)[:, seq - (ka - 1):])
        new_s_p.append(u_p.reshape(batch, seq, da)[:, seq - (kb - 1):])
        j = l // 2
        routed = l % 2 == 1
        rt = router[j] if routed else None
        outs_p = _mixer_prompt(glu_p, u_p, bg_p, x_p, conv_a_w, cab, lg, lb, conv_b_w, w_out_bf,
                               ffn_gain, rt, l, batch, seq)
        outs_s = _mixer_sample(glu_s, u_s, bg_s, x_s, state_a_t, state_s_t, conv_a_w, cab, lg,
                               lb, conv_b_w, w_out, ffn_gain, rt, outs_p[1] if routed else None,
                               n_p, l)
        glu_s_all.append(glu_s)
        u_s_all.append(u_s)
        x1_p, x1_s = outs_p[0], outs_s[0]
        if not routed:
            x_p, x_s = _dense_ffn(x1_p, x1_s, ffn_gain, dense_w_gate, dense_w_up, dense_w_down, l, j)
        else:
            top_idx = jnp.concatenate([outs_p[2], outs_s[2]], axis=0)
            gates = jnp.concatenate([outs_p[3], outs_s[3]], axis=0)
            h2 = outs_s[1]
            row_ids, slot_gate, pos, tile_expert, sub_active = _slot_layout(
                top_idx, gates, n_experts, MOE_BLOCKS)
            ys = _expert_ffn(row_ids, tile_expert, sub_active, h2, slot_gate,
                             exp_w_gate, exp_w_up, exp_w_down, j)
            normed = l == depth - 1
            x_p, x_s = _combine(pos, x1_p, x1_s, ys, final_gain, normed)
    if not normed:
        x_p, x_s = _final_norm(x_p, final_gain), _final_norm(x_s, final_gain)
    new_a_t, new_s_t = _state_update(state_a_t, state_s_t, jnp.stack(glu_s_all), jnp.stack(u_s_all))
    return (x_p.reshape(batch, seq, d), x_s.reshape(n_seq, 1, d), jnp.stack(new_a_p),
            jnp.stack(new_s_p), jnp.transpose(new_a_t, (0, 2, 1, 3)),
            jnp.transpose(new_s_t, (0, 2, 1, 3)))
```
